```python
import math
import jax, jax.numpy as jnp
from jax import lax
import numpy as np

D_MODEL = 1024
BATCH = 4
SEQ = 8192
DEPTH = 1
DEC_BATCH = 16
DEC_SEQ = 4096
PAST_LEN = 128

MIX_WIDTH = D_MODEL
HY_WIDTH = MIX_WIDTH // 2
ML_WIDTH = MIX_WIDTH - HY_WIDTH
ML_HEADS = 4
ML_HEAD_DIM = ML_WIDTH // ML_HEADS
ML_CHUNK = 64
HY_ORDER = 2
HY_EMB = 33
HY_BANDS = (HY_EMB - 1) // 2
HY_FILTER_HIDDEN = 64
HY_DECAY_TARGET = 1e-2
HY_FAST_DECAY_PCT = 0.3
HY_SLOW_DECAY_PCT = 1.5
HY_MAX_DECAY = math.log(HY_DECAY_TARGET) / HY_FAST_DECAY_PCT
HY_MIN_DECAY = math.log(HY_DECAY_TARGET) / HY_SLOW_DECAY_PCT
D_FF = 2816
EPS = 1e-6

OFF_Q = 3 * HY_WIDTH
OFF_K = OFF_Q + ML_WIDTH
OFF_V = OFF_K + ML_WIDTH
OFF_O = OFF_V + ML_WIDTH
OFF_G = OFF_O + ML_WIDTH
N_GATE = 4 * ML_HEADS
P_IN = OFF_G + N_GATE
CONV_CH = OFF_V

kernel_name = "hyena_mlstm_parallel_encoder"

F32 = jnp.float32


def rmsnorm(x, g):
    xf = x.astype(F32)
    y = xf * lax.rsqrt(jnp.mean(jnp.square(xf), axis=-1, keepdims=True) + EPS)
    return (y * g.astype(F32)).astype(x.dtype)


def dwconv3(x, w, b):
    xp = jnp.pad(x, ((0, 0), (1, 1), (0, 0)))
    return xp[:, :-2] * w[0] + xp[:, 1:-1] * w[1] + xp[:, 2:] * w[2] + b


def hyena_filter_spectrum(L, w1, b1, fr1, w2, b2, fr2, w3):
    t = jnp.linspace(0.0, 1.0, L, dtype=F32)[:, None]
    freqs = jnp.linspace(1e-4, HY_BANDS - 1, HY_BANDS, dtype=F32)
    ang = (2.0 * math.pi / L) * jnp.arange(L, dtype=F32)[:, None] * freqs[None, :]
    z = jnp.concatenate([t, jnp.cos(ang), -jnp.sin(ang)], axis=-1)
    h = jnp.sin(fr1.astype(F32) * (z @ w1.astype(F32) + b1.astype(F32)))
    h = jnp.sin(fr2.astype(F32) * (h @ w2.astype(F32) + b2.astype(F32)))
    h = (h @ w3.astype(F32)).reshape(L, 2, HY_ORDER, HY_WIDTH)
    deltas = jnp.abs(jnp.linspace(HY_MIN_DECAY, HY_MAX_DECAY, HY_WIDTH, dtype=F32))
    h = h * jnp.exp(-t * deltas)[:, None, None, :]
    kern = jnp.concatenate([h[:, 0], jnp.zeros((1, HY_ORDER, HY_WIDTH), F32), h[:0:-1, 1]], axis=0)
    kern = kern * lax.rsqrt(jnp.sum(jnp.square(kern), axis=0, keepdims=True))
    return jnp.fft.rfft(kern, axis=0)


def hyena_mix(x1, x2, v, kspec, bias):
    L = v.shape[1]
    bias = bias.astype(F32)

    def long_conv(z, ks, d):
        zf = jnp.fft.rfft(z, n=2 * L, axis=1)
        return jnp.fft.irfft(zf * ks[None], n=2 * L, axis=1)[:, :L] + d * z

    z = x1.astype(F32) * long_conv(v.astype(F32), kspec[:, 0], bias[0])
    return x2.astype(F32) * long_conv(z, kspec[:, 1], bias[1])


def mlstm_chunkwise(q, k, v, ig, fg):
    B, H, S, DH = q.shape
    NC = S // ML_CHUNK
    q = q.reshape(B, H, NC, ML_CHUNK, DH)
    k = k.reshape(B, H, NC, ML_CHUNK, DH)
    v = v.reshape(B, H, NC, ML_CHUNK, DH)
    ig = ig.reshape(B, H, NC, ML_CHUNK)
    b = jnp.cumsum(jax.nn.log_sigmoid(fg).reshape(B, H, NC, ML_CHUNK), axis=-1)
    b_last = b[..., -1]
    a = b_last[..., None] - b + ig
    m_loc = jnp.max(a, axis=-1)
    w_loc = jnp.exp(a - m_loc[..., None])
    kw = k * w_loc[..., None]
    C_loc = jnp.einsum('bhcsk,bhcsv->bhckv', kw, v)
    n_loc = jnp.sum(kw, axis=3)

    def step(carry, xs):
        C, n, m = carry
        C_l, n_l, m_l, bl = xs
        m_new = jnp.maximum(bl + m, m_l)
        s_old = jnp.exp(bl + m - m_new)
        s_loc = jnp.exp(m_l - m_new)
        C_new = s_old[..., None, None] * C + s_loc[..., None, None] * C_l
        n_new = s_old[..., None] * n + s_loc[..., None] * n_l
        return (C_new, n_new, m_new), (C, n, m)

    init = (jnp.zeros((B, H, DH, DH), F32), jnp.zeros((B, H, DH), F32), jnp.zeros((B, H), F32))
    xs = (jnp.moveaxis(C_loc, 2, 0), jnp.moveaxis(n_loc, 2, 0),
          jnp.moveaxis(m_loc, 2, 0), jnp.moveaxis(b_last, 2, 0))
    _, (C_prev, n_prev, m_prev) = lax.scan(step, init, xs)
    m_prev = jnp.transpose(m_prev, (1, 2, 0))

    g = b + m_prev[..., None]
    D = b[..., :, None] - b[..., None, :] + ig[..., None, :]
    mask = jnp.tril(jnp.ones((ML_CHUNK, ML_CHUNK), dtype=bool))
    D = jnp.where(mask, D, -jnp.inf)
    m_t = jnp.maximum(g, jnp.max(D, axis=-1))
    Sm = jnp.einsum('bhctd,bhcsd->bhcts', q, k) * jnp.exp(D - m_t[..., None])
    inter = jnp.exp(g - m_t)
    num = inter[..., None] * jnp.einsum('bhctk,cbhkv->bhctv', q, C_prev) \
        + jnp.einsum('bhcts,bhcsv->bhctv', Sm, v)
    den = inter * jnp.einsum('bhctk,cbhk->bhct', q, n_prev) + jnp.sum(Sm, axis=-1)
    h = num / jnp.maximum(jnp.abs(den), jnp.exp(-m_t))[..., None]
    return h.reshape(B, H, S, DH)


def mlstm_bidir(q, k, v, gates):
    B, S = q.shape[0], q.shape[1]
    g = gates.reshape(B, S, 4, ML_HEADS).transpose(2, 0, 3, 1)
    q, k, v = (a.transpose(0, 2, 1, 3) for a in (q, k, v))
    flip = lambda a: jnp.flip(a, axis=2)
    h_f = mlstm_chunkwise(q, k, v, g[0], g[1])
    h_b = flip(mlstm_chunkwise(flip(q), flip(k), flip(v), flip(g[2]), flip(g[3])))
    return (h_f + h_b).transpose(0, 2, 1, 3)


def encoder_layer(x, norm1_g, w_in, conv_w, conv_b, gate_b, filt_w1, filt_b1, filt_freq1,
                  filt_w2, filt_b2, filt_freq2, filt_w3, hy_bias, ml_norm_g, w_out,
                  norm2_g, w_up, ffn_conv_w, ffn_conv_b, w_down):
    B, S, _ = x.shape
    h = rmsnorm(x, norm1_g)
    proj = h @ w_in
    cp = dwconv3(proj[..., :CONV_CH], conv_w, conv_b)
    x1 = cp[..., :HY_WIDTH]
    x2 = cp[..., HY_WIDTH:2 * HY_WIDTH]
    hv = cp[..., 2 * HY_WIDTH:OFF_Q]
    kspec = hyena_filter_spectrum(S, filt_w1, filt_b1, filt_freq1, filt_w2, filt_b2, filt_freq2, filt_w3)
    y_hy = hyena_mix(x1, x2, hv, kspec, hy_bias).astype(x.dtype)
    q = jax.nn.silu(cp[..., OFF_Q:OFF_K].astype(F32)).reshape(B, S, ML_HEADS, ML_HEAD_DIM)
    k = (jax.nn.silu(cp[..., OFF_K:OFF_V].astype(F32)) * (ML_HEAD_DIM ** -0.5)).reshape(B, S, ML_HEADS, ML_HEAD_DIM)
    v = proj[..., OFF_V:OFF_O].astype(F32).reshape(B, S, ML_HEADS, ML_HEAD_DIM)
    o = jax.nn.sigmoid(proj[..., OFF_O:OFF_G].astype(F32))
    gates = proj[..., OFF_G:].astype(F32) + gate_b.astype(F32)
    hm = mlstm_bidir(q, k, v, gates)
    hm = hm * lax.rsqrt(jnp.mean(jnp.square(hm), axis=-1, keepdims=True) + EPS)
    hm = hm.reshape(B, S, ML_WIDTH) * ml_norm_g.astype(F32)
    y_ml = (o * hm).astype(x.dtype)
    x = x + jnp.concatenate([y_hy, y_ml], axis=-1) @ w_out
    h = rmsnorm(x, norm2_g)
    u = dwconv3(h @ w_up, ffn_conv_w, ffn_conv_b)
    val, gate = u[..., :D_FF], u[..., D_FF:]
    return x + (jax.nn.silu(gate) * val) @ w_down


def trunk(x, layer_params, final_g):
    for l in range(DEPTH):
        x = encoder_layer(x, *[p[l] for p in layer_params])
    return rmsnorm(x, final_g)


def setup_inputs(seed: int = 0) -> dict:
    key = jax.random.key(seed)
    ks = jax.random.split(key, 28)
    nrm = lambda k, shape, scale: jax.random.normal(k, shape, F32) * scale
    is_f = jnp.array([0.0, 1.0, 0.0, 1.0], F32)[None, :, None]
    gate_b = jnp.where(is_f > 0,
                       3.0 + 3.0 * jax.random.uniform(ks[5], (DEPTH, 4, ML_HEADS), F32),
                       0.1 * jax.random.normal(ks[6], (DEPTH, 4, ML_HEADS), F32)).reshape(DEPTH, N_GATE)
    return {
        "x_prompt": nrm(ks[0], (BATCH, SEQ, D_MODEL), 1.0),
        "x_sample": nrm(ks[1], (DEC_BATCH, DEC_SEQ, D_MODEL), 1.0),
        "norm1_g": 1.0 + nrm(ks[2], (DEPTH, D_MODEL), 0.02),
        "w_in": nrm(ks[3], (DEPTH, D_MODEL, P_IN), D_MODEL ** -0.5),
        "conv_w": nrm(ks[4], (DEPTH, 3, CONV_CH), 3 ** -0.5),
        "conv_b": nrm(ks[7], (DEPTH, CONV_CH), 0.02),
        "gate_b": gate_b,
        "filt_w1": nrm(ks[8], (DEPTH, HY_EMB, HY_FILTER_HIDDEN), HY_EMB ** -0.5),
        "filt_b1": nrm(ks[9], (DEPTH, HY_FILTER_HIDDEN), 0.1),
        "filt_freq1": 1.0 + nrm(ks[10], (DEPTH, HY_FILTER_HIDDEN), 0.1),
        "filt_w2": nrm(ks[11], (DEPTH, HY_FILTER_HIDDEN, HY_FILTER_HIDDEN), HY_FILTER_HIDDEN ** -0.5),
        "filt_b2": nrm(ks[12], (DEPTH, HY_FILTER_HIDDEN), 0.1),
        "filt_freq2": 1.0 + nrm(ks[13], (DEPTH, HY_FILTER_HIDDEN), 0.1),
        "filt_w3": nrm(ks[14], (DEPTH, HY_FILTER_HIDDEN, 2 * HY_ORDER * HY_WIDTH), HY_FILTER_HIDDEN ** -0.5),
        "hy_bias": nrm(ks[15], (DEPTH, HY_ORDER, HY_WIDTH), 0.1),
        "ml_norm_g": 1.0 + nrm(ks[16], (DEPTH, ML_WIDTH), 0.02),
        "w_out": nrm(ks[17], (DEPTH, MIX_WIDTH, D_MODEL), MIX_WIDTH ** -0.5),
        "norm2_g": 1.0 + nrm(ks[18], (DEPTH, D_MODEL), 0.02),
        "w_up": nrm(ks[19], (DEPTH, D_MODEL, 2 * D_FF), D_MODEL ** -0.5),
        "ffn_conv_w": nrm(ks[20], (DEPTH, 3, 2 * D_FF), 3 ** -0.5),
        "ffn_conv_b": nrm(ks[21], (DEPTH, 2 * D_FF), 0.02),
        "w_down": nrm(ks[22], (DEPTH, D_FF, D_MODEL), D_FF ** -0.5),
        "final_g": 1.0 + nrm(ks[23], (D_MODEL,), 0.02),
    }


def reference(x_prompt, x_sample, norm1_g, w_in, conv_w, conv_b, gate_b, filt_w1, filt_b1,
              filt_freq1, filt_w2, filt_b2, filt_freq2, filt_w3, hy_bias, ml_norm_g, w_out,
              norm2_g, w_up, ffn_conv_w, ffn_conv_b, w_down, final_g):
    layer_params = (norm1_g, w_in, conv_w, conv_b, gate_b, filt_w1, filt_b1, filt_freq1,
                    filt_w2, filt_b2, filt_freq2, filt_w3, hy_bias, ml_norm_g, w_out,
                    norm2_g, w_up, ffn_conv_w, ffn_conv_b, w_down)
    y_prompt = trunk(x_prompt, layer_params, final_g)
    y_sample = trunk(x_sample, layer_params, final_g)
    return (y_prompt, y_sample)
```

```python
import functools
import math

import jax
import jax.numpy as jnp
from jax import lax
from jax.experimental import pallas as pl
from jax.experimental.pallas import tpu as pltpu

F32 = jnp.float32
BF16 = jnp.bfloat16
HIGHEST = lax.Precision.HIGHEST

LANES = 128
SUBLANES = 8

D_MODEL = 1024
HY_WIDTH = 512
ML_WIDTH = 512
ML_HEADS = 4
ML_HEAD_DIM = ML_WIDTH // ML_HEADS
HY_ORDER = 2
HY_EMB = 33
HY_BANDS = (HY_EMB - 1) // 2
HY_HIDDEN = 64
HY_DECAY_TARGET = 1e-2
HY_MAX_DECAY = math.log(HY_DECAY_TARGET) / 0.3
HY_MIN_DECAY = math.log(HY_DECAY_TARGET) / 1.5
D_FF = 2816
EPS = 1e-6

HY_COLS = 3 * HY_WIDTH
OFF_Q = HY_COLS
OFF_K = OFF_Q + ML_WIDTH
OFF_V = OFF_K + ML_WIDTH
OFF_O = OFF_V + ML_WIDTH
OFF_G = OFF_O + ML_WIDTH
N_GATE = 4 * ML_HEADS
CONV_CH = OFF_V

ML_CHUNK = 256
IN_TM = 512
OUT_TM = 1024
FFN_TM = 512
FFN_FT = 256
FILT_PC = 1024
FFT_ROWS = 2048


def _rms(xv, g):
    ms = jnp.mean(xv * xv, axis=-1, keepdims=True)
    return xv * lax.rsqrt(ms + EPS) * g


def _sigmoid(x):
    return 1.0 / (1.0 + jnp.exp(-x))


def _conv3_rows(p, w, b, tm):
    n = p.shape[0]
    dn = pltpu.roll(p, 1, 0)[SUBLANES:tm + SUBLANES]
    up = pltpu.roll(p, n - 1, 0)[SUBLANES:tm + SUBLANES]
    mid = p[SUBLANES:tm + SUBLANES]
    return dn * w[0:1] + mid * w[1:2] + up * w[2:3] + b


def _in_proj_body(xp_ref, x_ref, xn_ref, g1_ref, whyT_ref, wqk_ref, wvo_ref, cw_ref, cb_ref, gb_ref,
                  hyT_ref, gT_ref, q_ref, k_ref, v_ref, o_ref, *, tm):
    i = pl.program_id(1)
    nt = pl.num_programs(1)
    g1 = g1_ref[...]
    h = _rms(x_ref[0], g1)
    hp = jnp.where(i > 0, _rms(xp_ref[0], g1), 0.0)
    hn = jnp.where(i < nt - 1, _rms(xn_ref[0], g1), 0.0)
    hb = h.astype(BF16)
    rT = lax.dot_general(whyT_ref[...], hb, (((1,), (1,)), ((), ())), preferred_element_type=F32)
    for j in range(tm // LANES):
        hyT_ref[0, j] = rT[:HY_COLS, j * LANES:(j + 1) * LANES]
    gT_ref[0] = rT[HY_COLS:, :] + gb_ref[...]
    hext = jnp.concatenate([hp, h, hn], axis=0).astype(BF16)
    pqk = jnp.dot(hext, wqk_ref[...], preferred_element_type=F32)
    c = _conv3_rows(pqk, cw_ref[...], cb_ref[...], tm)
    qk = c * _sigmoid(c)
    q_ref[0] = qk[:, :ML_WIDTH].astype(BF16)
    k_ref[0] = (qk[:, ML_WIDTH:] * (ML_HEAD_DIM ** -0.5)).astype(BF16)
    pvo = jnp.dot(hb, wvo_ref[...], preferred_element_type=F32)
    v_ref[0] = pvo[:, :ML_WIDTH].astype(BF16)
    o_ref[0] = _sigmoid(pvo[:, ML_WIDTH:])


def _in_proj(x, g1, whyT, wqk, wvo, cw_qk, cb_qk, gb):
    B, S, D = x.shape
    tm = min(IN_TM, S)
    nt = S // tm
    r8 = tm // SUBLANES
    full = lambda shape: pl.BlockSpec(shape, lambda b, i: (0,) * len(shape))
    return pl.pallas_call(
        functools.partial(_in_proj_body, tm=tm),
        grid=(B, nt),
        in_specs=[
            pl.BlockSpec((1, SUBLANES, D), lambda b, i: (b, jnp.maximum(i * r8 - 1, 0), 0)),
            pl.BlockSpec((1, tm, D), lambda b, i: (b, i, 0)),
            pl.BlockSpec((1, SUBLANES, D), lambda b, i: (b, jnp.minimum((i + 1) * r8, S // SUBLANES - 1), 0)),
            full((1, D)), full(whyT.shape), full(wqk.shape), full(wvo.shape),
            full(cw_qk.shape), full(cb_qk.shape), full(gb.shape),
        ],
        out_specs=[
            pl.BlockSpec((1, tm // LANES, HY_COLS, LANES), lambda b, i: (b, i, 0, 0)),
            pl.BlockSpec((1, N_GATE, tm), lambda b, i: (b, 0, i)),
            pl.BlockSpec((1, tm, ML_WIDTH), lambda b, i: (b, i, 0)),
            pl.BlockSpec((1, tm, ML_WIDTH), lambda b, i: (b, i, 0)),
            pl.BlockSpec((1, tm, ML_WIDTH), lambda b, i: (b, i, 0)),
            pl.BlockSpec((1, tm, ML_WIDTH), lambda b, i: (b, i, 0)),
        ],
        out_shape=[
            jax.ShapeDtypeStruct((B, S // LANES, HY_COLS, LANES), F32),
            jax.ShapeDtypeStruct((B, N_GATE, S), F32),
            jax.ShapeDtypeStruct((B, S, ML_WIDTH), BF16),
            jax.ShapeDtypeStruct((B, S, ML_WIDTH), BF16),
            jax.ShapeDtypeStruct((B, S, ML_WIDTH), BF16),
            jax.ShapeDtypeStruct((B, S, ML_WIDTH), F32),
        ],
        name="in_proj",
    )(x, x, x, g1, whyT, wqk, wvo, cw_qk, cb_qk, gb)


def _dft_tables(L):
    N = 2 * L
    N1 = N // LANES
    H1 = N1 // 2

    def cs(num, den):
        ang = (2.0 * math.pi / den) * (num % den).astype(F32)
        return jnp.cos(ang), jnp.sin(ang)

    k1 = jnp.arange(N1, dtype=jnp.int32)[:, None]
    c, s = cs(k1 * jnp.arange(H1, dtype=jnp.int32)[None, :], N1)
    a1 = jnp.block([[c, s], [-s, c]])
    a1i = jnp.block([[c.T, -s.T], [s.T, c.T]]) * (1.0 / N)
    cf, sf = cs(k1 * jnp.arange(N1, dtype=jnp.int32)[None, :], N1)
    a1k = jnp.concatenate([cf, -sf], axis=0)
    ct, st = cs(k1 * jnp.arange(LANES, dtype=jnp.int32)[None, :], N)
    twr, twi = ct, -st
    n2 = jnp.arange(LANES, dtype=jnp.int32)
    cg, sg = cs(n2[:, None] * n2[None, :], LANES)
    b2 = jnp.block([[cg, -sg], [sg, cg]])
    b2c = jnp.block([[cg, sg], [-sg, cg]])
    return dict(N1=N1, H1=H1, a1=a1, a1i=a1i, a1k=a1k, twr=twr, twi=twi, b2=b2, b2c=b2c)


def _filter_features(L):
    t = jnp.linspace(0.0, 1.0, L, dtype=F32)[:, None]
    freqs = jnp.linspace(1e-4, HY_BANDS - 1, HY_BANDS, dtype=F32)
    ang = (2.0 * math.pi / L) * jnp.arange(L, dtype=F32)[:, None] * freqs[None, :]
    z = jnp.concatenate([t, jnp.cos(ang), -jnp.sin(ang)], axis=-1)
    z_all = jnp.concatenate([z, z[0:1], z[L - 1:0:-1]], axis=0)
    t_all = jnp.concatenate([t, t[0:1], t[L - 1:0:-1]], axis=0)
    zT = jnp.pad(z_all, ((0, 0), (0, HY_HIDDEN - HY_EMB))).T
    return zT, t_all.T


def _filt_mlp_body(zT_ref, w1T_ref, b1_ref, f1_ref, w2T_ref, b2_ref, f2_ref, h2_ref):
    h1 = jnp.sin(f1_ref[...] * (jnp.dot(w1T_ref[...], zT_ref[...], preferred_element_type=F32,
                                        precision=HIGHEST) + b1_ref[...]))
    h2_ref[...] = jnp.sin(f2_ref[...] * (jnp.dot(w2T_ref[...], h1, preferred_element_type=F32,
                                                 precision=HIGHEST) + b2_ref[...]))


def _filt_body(h2_ref, t_ref, w3T_ref, dl_ref, kT_ref, sc_ref, *, L, pc):
    p = pl.program_id(1)
    last = pl.num_programs(1) - 1
    w3 = w3T_ref[0].reshape(HY_ORDER * LANES, HY_HIDDEN)
    h3 = jnp.dot(w3, h2_ref[...], preferred_element_type=F32, precision=HIGHEST)
    win = jnp.exp(-t_ref[...] * dl_ref[...])
    n_idx = p * pc + lax.broadcasted_iota(jnp.int32, (1, pc), 1)
    win = jnp.where(n_idx == L, 0.0, win)
    kern = h3 * jnp.concatenate([win] * HY_ORDER, axis=0)
    for o in range(HY_ORDER):
        for j in range(pc // LANES):
            kT_ref[o, j] = kern[o * LANES:(o + 1) * LANES, j * LANES:(j + 1) * LANES]
    ss = jnp.sum(kern * kern, axis=1, keepdims=True)

    @pl.when(p == 0)
    def _():
        sc_ref[0] = ss

    @pl.when(jnp.logical_and(p > 0, p < last))
    def _():
        sc_ref[0] = sc_ref[0] + ss

    @pl.when(p == last)
    def _():
        sc_ref[0] = lax.rsqrt(sc_ref[0] + ss)


def _filt(L, zT, t_all, w1T, b1, f1, w2T, b2, f2, w3T, dl):
    pc = min(FILT_PC, L)
    npc = 2 * L // pc
    ncb = HY_WIDTH // LANES
    full1 = lambda a: pl.BlockSpec(a.shape, lambda p: (0,) * a.ndim)
    h2T = pl.pallas_call(
        _filt_mlp_body,
        grid=(npc,),
        in_specs=[pl.BlockSpec((HY_HIDDEN, pc), lambda p: (0, p)),
                  full1(w1T), full1(b1), full1(f1), full1(w2T), full1(b2), full1(f2)],
        out_specs=pl.BlockSpec((HY_HIDDEN, pc), lambda p: (0, p)),
        out_shape=jax.ShapeDtypeStruct((HY_HIDDEN, 2 * L), F32),
        name="hyena_filter_mlp",
    )(zT, w1T, b1, f1, w2T, b2, f2)
    return pl.pallas_call(
        functools.partial(_filt_body, L=L, pc=pc),
        grid=(ncb, npc),
        in_specs=[
            pl.BlockSpec((HY_HIDDEN, pc), lambda c, p: (0, p)),
            pl.BlockSpec((1, pc), lambda c, p: (0, p)),
            pl.BlockSpec((1, HY_ORDER, LANES, HY_HIDDEN), lambda c, p: ((p * pc) // L, 0, c, 0)),
            pl.BlockSpec((LANES, 1), lambda c, p: (c, 0)),
        ],
        out_specs=[
            pl.BlockSpec((HY_ORDER, pc // LANES, LANES, LANES), lambda c, p: (0, p, c, 0)),
            pl.BlockSpec((1, HY_ORDER * LANES, 1), lambda c, p: (c, 0, 0)),
        ],
        out_shape=[
            jax.ShapeDtypeStruct((HY_ORDER, 2 * L // LANES, HY_WIDTH, LANES), F32),
            jax.ShapeDtypeStruct((ncb, HY_ORDER * LANES, 1), F32),
        ],
        name="hyena_filter",
    )(h2T, t_all, w3T, dl)


def _kfft_body(sc_ref, kT_ref, a1k_ref, twr_ref, twi_ref, b2_ref, kr_ref, ki_ref, x_s, l2_s, *, N1, cbn):
    o = pl.program_id(0)
    cb = pl.program_id(1)
    for c in range(cbn):
        s = sc_ref[o * HY_WIDTH + cb * cbn + c]
        x_s[:, c * LANES:(c + 1) * LANES] = kT_ref[:, c, :] * s
    y = jnp.dot(a1k_ref[...], x_s[...], preferred_element_type=F32, precision=HIGHEST)
    twr = twr_ref[...]
    twi = twi_ref[...]
    for c in range(cbn):
        yr = y[0:N1, c * LANES:(c + 1) * LANES]
        yi = y[N1:2 * N1, c * LANES:(c + 1) * LANES]
        l2_s[c * N1:(c + 1) * N1, 0:LANES] = yr * twr - yi * twi
        l2_s[c * N1:(c + 1) * N1, LANES:2 * LANES] = yr * twi + yi * twr
    z = jnp.dot(l2_s[...], b2_ref[...], preferred_element_type=F32, precision=HIGHEST)
    kr_ref[...] = z[:, :LANES]
    ki_ref[...] = z[:, LANES:]


def _kfft(scale, kT, tabs):
    N1 = tabs["N1"]
    cbn = min(FFT_ROWS // N1, 32)
    ncb = HY_WIDTH // cbn
    full = lambda a: pl.BlockSpec(a.shape, lambda o, c: (0,) * a.ndim)
    a1k, twr, twi, b2 = tabs["a1k"], tabs["twr"], tabs["twi"], tabs["b2"]
    return pl.pallas_call(
        functools.partial(_kfft_body, N1=N1, cbn=cbn),
        grid=(HY_ORDER, ncb),
        in_specs=[
            pl.BlockSpec(memory_space=pltpu.SMEM),
            pl.BlockSpec((None, N1, cbn, LANES), lambda o, c: (o, 0, c, 0)),
            full(a1k), full(twr), full(twi), full(b2),
        ],
        out_specs=[
            pl.BlockSpec((None, cbn * N1, LANES), lambda o, c: (o, c, 0)),
            pl.BlockSpec((None, cbn * N1, LANES), lambda o, c: (o, c, 0)),
        ],
        out_shape=[
            jax.ShapeDtypeStruct((HY_ORDER, HY_WIDTH * N1, LANES), F32),
            jax.ShapeDtypeStruct((HY_ORDER, HY_WIDTH * N1, LANES), F32),
        ],
        scratch_shapes=[pltpu.VMEM((N1, cbn * LANES), F32), pltpu.VMEM((cbn * N1, 2 * LANES), F32)],
        name="hyena_kfft",
    )(scale, kT, a1k, twr, twi, b2)


def _hyena_body(cw_ref, cbias_ref, hyb_ref, x1_ref, x2_ref, v_ref, kr_ref, ki_ref, a1_ref, twr_ref, twi_ref,
                b2_ref, b2c_ref, a1i_ref, o_ref, x_s, l2_s, r1_s, z_s, *, N1, H1, cbn):
    cb = pl.program_id(0)
    lane = lax.broadcasted_iota(jnp.int32, (H1, LANES), 1)
    row = lax.broadcasted_iota(jnp.int32, (H1, LANES), 0)
    lane_first = lane == 0
    lane_last = lane == LANES - 1
    t_first = jnp.logical_and(row == 0, lane_first)
    t_last = jnp.logical_and(row == H1 - 1, lane_last)

    def conv_in(ref, e, c, off):
        ch = off + cb * cbn + c
        x = ref[e, :, c, :]
        r = pltpu.roll(x, 1, 1)
        prev = jnp.where(lane_first, pltpu.roll(r, 1, 0), r)
        prev = jnp.where(t_first, 0.0, prev)
        l = pltpu.roll(x, LANES - 1, 1)
        nxt = jnp.where(lane_last, pltpu.roll(l, H1 - 1, 0), l)
        nxt = jnp.where(t_last, 0.0, nxt)
        return (prev * cw_ref[ch] + x * cw_ref[CONV_CH + ch] + nxt * cw_ref[2 * CONV_CH + ch]
                + cbias_ref[ch])

    def long_conv(order):
        twr = twr_ref[...]
        twi = twi_ref[...]
        y = jnp.dot(a1_ref[...], x_s[...], preferred_element_type=F32)
        for c in range(cbn):
            yr = y[0:N1, c * LANES:(c + 1) * LANES]
            yi = y[N1:2 * N1, c * LANES:(c + 1) * LANES]
            l2_s[c * N1:(c + 1) * N1, 0:LANES] = (yr * twr - yi * twi).astype(BF16)
            l2_s[c * N1:(c + 1) * N1, LANES:2 * LANES] = (yr * twi + yi * twr).astype(BF16)
        z = jnp.dot(l2_s[...], b2_ref[...], preferred_element_type=F32)
        zr = z[:, :LANES]
        zi = z[:, LANES:]
        kr = kr_ref[order]
        ki = ki_ref[order]
        l2_s[:, 0:LANES] = (zr * kr - zi * ki).astype(BF16)
        l2_s[:, LANES:2 * LANES] = (zr * ki + zi * kr).astype(BF16)
        v = jnp.dot(l2_s[...], b2c_ref[...], preferred_element_type=F32)
        for c in range(cbn):
            vr = v[c * N1:(c + 1) * N1, :LANES]
            vi = v[c * N1:(c + 1) * N1, LANES:]
            r1_s[0:N1, c * LANES:(c + 1) * LANES] = (vr * twr + vi * twi).astype(BF16)
            r1_s[N1:2 * N1, c * LANES:(c + 1) * LANES] = (vi * twr - vr * twi).astype(BF16)
        return jnp.dot(a1i_ref[...], r1_s[...], preferred_element_type=F32)

    for c in range(cbn):
        for e in range(2):
            z1 = conv_in(v_ref, e, c, 2 * HY_WIDTH)
            z_s[e, c] = z1
            x_s[e * H1:(e + 1) * H1, c * LANES:(c + 1) * LANES] = z1.astype(BF16)
    y1 = long_conv(0)
    for c in range(cbn):
        ch = cb * cbn + c
        for e in range(2):
            c1 = y1[e * H1:(e + 1) * H1, c * LANES:(c + 1) * LANES] + hyb_ref[ch] * z_s[e, c]
            z2 = conv_in(x1_ref, e, c, 0) * c1
            z_s[e, c] = z2
            x_s[e * H1:(e + 1) * H1, c * LANES:(c + 1) * LANES] = z2.astype(BF16)
    y2 = long_conv(1)
    for c in range(cbn):
        ch = cb * cbn + c
        for e in range(2):
            c2 = y2[e * H1:(e + 1) * H1, c * LANES:(c + 1) * LANES] + hyb_ref[HY_WIDTH + ch] * z_s[e, c]
            o_ref[e, c] = conv_in(x2_ref, e, c, HY_WIDTH) * c2


def _hyena(hyT, cw, cbias, hyb, kr, ki, tabs):
    B, H1, _, _ = hyT.shape
    N1 = tabs["N1"]
    assert H1 * 2 == N1 and B % 2 == 0
    cbn = min(FFT_ROWS // N1, 32)
    ncb = HY_WIDTH // cbn
    hy5 = hyT.reshape(B // 2, 2, H1, HY_COLS, LANES)
    a1 = tabs["a1"].astype(BF16)
    a1i = tabs["a1i"].astype(BF16)
    b2 = tabs["b2"].astype(BF16)
    b2c = tabs["b2c"].astype(BF16)
    twr, twi = tabs["twr"], tabs["twi"]
    full = lambda a: pl.BlockSpec(a.shape, lambda c, p: (0,) * a.ndim)
    smem = pl.BlockSpec(memory_space=pltpu.SMEM)
    xspec = lambda k: pl.BlockSpec((None, 2, H1, cbn, LANES), lambda c, p: (p, 0, 0, k * ncb + c, 0))
    kspec = pl.BlockSpec((HY_ORDER, cbn * N1, LANES), lambda c, p: (0, c, 0))
    out = pl.pallas_call(
        functools.partial(_hyena_body, N1=N1, H1=H1, cbn=cbn),
        grid=(ncb, B // 2),
        in_specs=[smem, smem, smem, xspec(0), xspec(1), xspec(2), kspec, kspec,
                  full(a1), full(twr), full(twi), full(b2), full(b2c), full(a1i)],
        out_specs=pl.BlockSpec((None, 2, cbn, H1, LANES), lambda c, p: (p, 0, c, 0, 0)),
        out_shape=jax.ShapeDtypeStruct((B // 2, 2, HY_WIDTH, H1, LANES), F32),
        scratch_shapes=[
            pltpu.VMEM((2 * H1, cbn * LANES), BF16),
            pltpu.VMEM((cbn * N1, 2 * LANES), BF16),
            pltpu.VMEM((2 * N1, cbn * LANES), BF16),
            pltpu.VMEM((2, cbn, H1, LANES), F32),
        ],
        name="hyena_mix",
    )(cw, cbias, hyb, hy5, hy5, hy5, kr, ki, a1, twr, twi, b2, b2c, a1i)
    return out.reshape(B, HY_WIDTH, H1, LANES)


def _scan_lanes(x, op, reverse, fill):
    T = x.shape[1]
    lane = lax.broadcasted_iota(jnp.int32, x.shape, 1)
    k = 1
    while k < T:
        if reverse:
            x = op(x, jnp.where(lane < T - k, pltpu.roll(x, T - k, 1), fill))
        else:
            x = op(x, jnp.where(lane >= k, pltpu.roll(x, k, 1), fill))
        k *= 2
    return x


def _log_sigmoid(x):
    return jnp.minimum(x, 0.0) - jnp.log(1.0 + jnp.exp(-jnp.abs(x)))


def _gate_rows(g, d, T):
    ig = g[8 * d:8 * d + ML_HEADS]
    fg = g[8 * d + ML_HEADS:8 * d + 2 * ML_HEADS]
    b = _scan_lanes(_log_sigmoid(fg), jnp.add, d == 1, 0.0)
    btot = b[:, 0:1] if d == 1 else b[:, T - 1:T]
    return b, ig - b, btot


def _ones_col(T):
    return (lax.broadcasted_iota(jnp.int32, (T, LANES), 1) == 0).astype(BF16)


def _mlstm_state_body(gf_ref, gb_ref, kf_ref, vf_ref, kb_ref, vb_ref, sf_ref, mf_ref, sb_ref, mb_ref,
                      s_s, m_s, *, T):
    i = pl.program_id(1)

    @pl.when(i == 0)
    def _():
        s_s[...] = jnp.zeros(s_s.shape, F32)
        m_s[...] = jnp.zeros(m_s.shape, F32)

    ones = _ones_col(T)
    for d, (g_ref, k_ref, v_ref, s_out, m_out) in enumerate(
            ((gf_ref, kf_ref, vf_ref, sf_ref, mf_ref), (gb_ref, kb_ref, vb_ref, sb_ref, mb_ref))):
        _, a, btot = _gate_rows(g_ref[0], d, T)
        amax = jnp.max(a, axis=1, keepdims=True)
        w = jnp.exp(a - amax)
        m_old = m_s[d, 0:ML_HEADS, 0:1]
        m_out[0, 0] = m_s[d]
        m_new = jnp.maximum(btot + m_old, btot + amax)
        s_old = jnp.exp(btot + m_old - m_new)
        s_loc = jnp.exp(btot + amax - m_new)
        m_s[d, 0:ML_HEADS, :] = jnp.broadcast_to(m_new, (ML_HEADS, LANES))
        for h in range(ML_HEADS):
            hs = slice(h * ML_HEAD_DIM, (h + 1) * ML_HEAD_DIM)
            kT = k_ref[0, :, hs].astype(F32).T
            kwT = (kT * w[h:h + 1, :]).astype(BF16)
            v1 = jnp.concatenate([v_ref[0, :, hs], ones], axis=1)
            c_loc = jnp.dot(kwT, v1, preferred_element_type=F32)
            st = s_s[d, h]
            s_out[0, 0, h] = st.astype(BF16)
            s_s[d, h] = s_old[h:h + 1, :] * st + s_loc[h:h + 1, :] * c_loc


def _mlstm_state(gT, k, v):
    B, S, _ = k.shape
    T = min(ML_CHUNK, S)
    nc = S // T
    sd = 2 * ML_HEAD_DIM
    gspec = lambda rev: pl.BlockSpec((1, N_GATE, T), (lambda b, i: (b, 0, nc - 1 - i)) if rev else (lambda b, i: (b, 0, i)))
    xspec = lambda rev: pl.BlockSpec((1, T, ML_WIDTH), (lambda b, i: (b, nc - 1 - i, 0)) if rev else (lambda b, i: (b, i, 0)))
    sspec = lambda rev: pl.BlockSpec((1, 1, ML_HEADS, ML_HEAD_DIM, sd),
                                     (lambda b, i: (b, nc - 1 - i, 0, 0, 0)) if rev else (lambda b, i: (b, i, 0, 0, 0)))
    mspec = lambda rev: pl.BlockSpec((1, 1, SUBLANES, LANES),
                                     (lambda b, i: (b, nc - 1 - i, 0, 0)) if rev else (lambda b, i: (b, i, 0, 0)))
    return pl.pallas_call(
        functools.partial(_mlstm_state_body, T=T),
        grid=(B, nc),
        in_specs=[gspec(False), gspec(True), xspec(False), xspec(False), xspec(True), xspec(True)],
        out_specs=[sspec(False), mspec(False), sspec(True), mspec(True)],
        out_shape=[
            jax.ShapeDtypeStruct((B, nc, ML_HEADS, ML_HEAD_DIM, sd), BF16),
            jax.ShapeDtypeStruct((B, nc, SUBLANES, LANES), F32),
            jax.ShapeDtypeStruct((B, nc, ML_HEADS, ML_HEAD_DIM, sd), BF16),
            jax.ShapeDtypeStruct((B, nc, SUBLANES, LANES), F32),
        ],
        scratch_shapes=[pltpu.VMEM((2, ML_HEADS, ML_HEAD_DIM, sd), F32), pltpu.VMEM((2, SUBLANES, LANES), F32)],
        name="mlstm_state",
    )(gT, gT, k, v, k, v)


def _mlstm_out_body(g_ref, q_ref, k_ref, v_ref, o_ref, sf_ref, mf_ref, sb_ref, mb_ref, ng_ref, y_ref, *, T):
    g = g_ref[0]
    a_rows = []
    cols = []
    for d, m_ref in enumerate((mf_ref, mb_ref)):
        b, a, _ = _gate_rows(g, d, T)
        m = m_ref[0, 0, 0:ML_HEADS, 0:1]
        mt = jnp.maximum(_scan_lanes(a, jnp.maximum, d == 1, -jnp.inf), m)
        a_rows.append(a)
        cols.append((mt, jnp.exp(-(b + mt)), jnp.exp(m - mt)))
    stacked = jnp.concatenate(
        [cols[0][0], cols[1][0], cols[0][1], cols[1][1], cols[0][2], cols[1][2],
         jnp.zeros((LANES - 6 * ML_HEADS, T), F32)], axis=0)
    colsT = stacked.T
    tt = lax.broadcasted_iota(jnp.int32, (T, T), 0)
    ss = lax.broadcasted_iota(jnp.int32, (T, T), 1)
    masks = (ss <= tt, ss >= tt)
    ones = _ones_col(T)
    s_refs = (sf_ref, sb_ref)
    for h in range(ML_HEADS):
        hs = slice(h * ML_HEAD_DIM, (h + 1) * ML_HEAD_DIM)
        qh = q_ref[0, :, hs]
        kh = k_ref[0, :, hs]
        v1 = jnp.concatenate([v_ref[0, :, hs], ones], axis=1)
        p = lax.dot_general(qh, kh, (((1,), (1,)), ((), ())), preferred_element_type=F32)
        hm = None
        for d in range(2):
            mt_c = colsT[:, 4 * d + h:4 * d + h + 1]
            emt_c = colsT[:, 8 + 4 * d + h:8 + 4 * d + h + 1]
            int_c = colsT[:, 16 + 4 * d + h:16 + 4 * d + h + 1]
            e = jnp.exp(jnp.where(masks[d], a_rows[d][h:h + 1, :] - mt_c, -jnp.inf))
            sm = (p * e).astype(BF16)
            r = (jnp.dot(sm, v1, preferred_element_type=F32)
                 + int_c * jnp.dot(qh, s_refs[d][0, 0, h], preferred_element_type=F32))
            num = r[:, :ML_HEAD_DIM]
            den = r[:, ML_HEAD_DIM:ML_HEAD_DIM + 1]
            hd = num / jnp.maximum(jnp.abs(den), emt_c)
            hm = hd if hm is None else hm + hd
        hn = hm * lax.rsqrt(jnp.mean(hm * hm, axis=-1, keepdims=True) + EPS)
        y_ref[0, :, hs] = hn * ng_ref[:, hs] * o_ref[0, :, hs]


def _mlstm_out(gT, q, k, v, o, sf, mf, sb, mb, ng):
    B, S, _ = k.shape
    T = min(ML_CHUNK, S)
    nc = S // T
    sd = 2 * ML_HEAD_DIM
    xspec = pl.BlockSpec((1, T, ML_WIDTH), lambda b, i: (b, i, 0))
    sspec = pl.BlockSpec((1, 1, ML_HEADS, ML_HEAD_DIM, sd), lambda b, i: (b, i, 0, 0, 0))
    mspec = pl.BlockSpec((1, 1, SUBLANES, LANES), lambda b, i: (b, i, 0, 0))
    return pl.pallas_call(
        functools.partial(_mlstm_out_body, T=T),
        grid=(B, nc),
        in_specs=[pl.BlockSpec((1, N_GATE, T), lambda b, i: (b, 0, i)), xspec, xspec, xspec, xspec,
                  sspec, mspec, sspec, mspec, pl.BlockSpec((1, ML_WIDTH), lambda b, i: (0, 0))],
        out_specs=xspec,
        out_shape=jax.ShapeDtypeStruct((B, S, ML_WIDTH), F32),
        name="mlstm_out",
    )(gT, q, k, v, o, sf, mf, sb, mb, ng)


def _out_proj_body(x_ref, yhy_ref, yml_ref, why_ref, wml_ref, o_ref, *, tm):
    yt = jnp.concatenate([yhy_ref[0, :, j, :].T for j in range(tm // LANES)], axis=0)
    acc = jnp.dot(yt.astype(BF16), why_ref[...], preferred_element_type=F32)
    acc = acc + jnp.dot(yml_ref[0].astype(BF16), wml_ref[...], preferred_element_type=F32)
    o_ref[0] = x_ref[0] + acc


def _out_proj(x, yhy, yml, why, wml):
    B, S, D = x.shape
    tm = min(OUT_TM, S)
    full = lambda a: pl.BlockSpec(a.shape, lambda b, i: (0,) * a.ndim)
    return pl.pallas_call(
        functools.partial(_out_proj_body, tm=tm),
        grid=(B, S // tm),
        in_specs=[
            pl.BlockSpec((1, tm, D), lambda b, i: (b, i, 0)),
            pl.BlockSpec((1, HY_WIDTH, tm // LANES, LANES), lambda b, i: (b, 0, i, 0)),
            pl.BlockSpec((1, tm, ML_WIDTH), lambda b, i: (b, i, 0)),
            full(why), full(wml),
        ],
        out_specs=pl.BlockSpec((1, tm, D), lambda b, i: (b, i, 0)),
        out_shape=jax.ShapeDtypeStruct((B, S, D), F32),
        name="out_proj",
    )(x, yhy, yml, why, wml)


def _ffn_body(xp_ref, x_ref, xn_ref, g2_ref, wup_ref, cw_ref, cb_ref, wdn_ref, gf_ref, o_ref, act_s,
              *, tm, ft, final_norm):
    i = pl.program_id(1)
    nt = pl.num_programs(1)
    g2 = g2_ref[...]
    xm = x_ref[0]
    h = _rms(xm, g2)
    hp = jnp.where(i > 0, _rms(xp_ref[0], g2), 0.0)
    hn = jnp.where(i < nt - 1, _rms(xn_ref[0], g2), 0.0)
    hext = jnp.concatenate([hp, h, hn], axis=0).astype(BF16)
    for j in range(D_FF // ft):
        cv = slice(j * ft, (j + 1) * ft)
        cg = slice(D_FF + j * ft, D_FF + (j + 1) * ft)
        uv = jnp.dot(hext, wup_ref[:, cv], preferred_element_type=F32)
        ug = jnp.dot(hext, wup_ref[:, cg], preferred_element_type=F32)
        val = _conv3_rows(uv, cw_ref[:, cv], cb_ref[:, cv], tm)
        gate = _conv3_rows(ug, cw_ref[:, cg], cb_ref[:, cg], tm)
        act_s[:, cv] = (gate * _sigmoid(gate) * val).astype(BF16)
    xo = xm + jnp.dot(act_s[...], wdn_ref[...], preferred_element_type=F32)
    o_ref[0] = _rms(xo, gf_ref[...]) if final_norm else xo


def _ffn(x, g2, wup, cw, cb, wdn, gf, final_norm):
    B, S, D = x.shape
    tm = min(FFN_TM, S)
    nt = S // tm
    r8 = tm // SUBLANES
    full = lambda a: pl.BlockSpec(a.shape, lambda b, i: (0,) * a.ndim)
    once = lambda a: pl.BlockSpec(a.shape, lambda b, i: (0,) * a.ndim, pipeline_mode=pl.Buffered(1))
    return pl.pallas_call(
        functools.partial(_ffn_body, tm=tm, ft=FFN_FT, final_norm=final_norm),
        grid=(B, nt),
        in_specs=[
            pl.BlockSpec((1, SUBLANES, D), lambda b, i: (b, jnp.maximum(i * r8 - 1, 0), 0)),
            pl.BlockSpec((1, tm, D), lambda b, i: (b, i, 0)),
            pl.BlockSpec((1, SUBLANES, D), lambda b, i: (b, jnp.minimum((i + 1) * r8, S // SUBLANES - 1), 0)),
            full(g2), once(wup), full(cw), full(cb), once(wdn), full(gf),
        ],
        out_specs=pl.BlockSpec((1, tm, D), lambda b, i: (b, i, 0)),
        out_shape=jax.ShapeDtypeStruct((B, S, D), F32),
        scratch_shapes=[pltpu.VMEM((tm, D_FF), BF16)],
        name="ffn",
    )(x, x, x, g2, wup, cw, cb, wdn, gf)


def _layer_params(l, norm1_g, w_in, conv_w, conv_b, gate_b, filt_w1, filt_b1, filt_freq1, filt_w2, filt_b2,
                  filt_freq2, filt_w3, hy_bias, ml_norm_g, w_out, norm2_g, w_up, ffn_conv_w, ffn_conv_b, w_down):
    w = w_in[l]
    col = lambda a: a.astype(F32).reshape(-1, 1)
    return dict(
        g1=norm1_g[l].reshape(1, D_MODEL),
        whyT=jnp.concatenate([w[:, :HY_COLS], w[:, OFF_G:]], axis=1).T.astype(BF16),
        wqk=w[:, OFF_Q:OFF_V].astype(BF16),
        wvo=w[:, OFF_V:OFF_G].astype(BF16),
        cw_qk=conv_w[l][:, OFF_Q:OFF_V], cb_qk=conv_b[l][OFF_Q:OFF_V].reshape(1, -1),
        gb=col(gate_b[l]),
        cw=conv_w[l].reshape(-1), cbias=conv_b[l], hyb=hy_bias[l].reshape(-1),
        w1T=jnp.pad(filt_w1[l], ((0, HY_HIDDEN - HY_EMB), (0, 0))).T, b1=col(filt_b1[l]), f1=col(filt_freq1[l]),
        w2T=filt_w2[l].T, b2=col(filt_b2[l]), f2=col(filt_freq2[l]),
        w3T=filt_w3[l].T.reshape(2, HY_ORDER, HY_WIDTH, HY_HIDDEN),
        ng=ml_norm_g[l].reshape(1, ML_WIDTH),
        why_o=w_out[l][:HY_WIDTH].astype(BF16), wml_o=w_out[l][HY_WIDTH:].astype(BF16),
        g2=norm2_g[l].reshape(1, D_MODEL),
        wup=w_up[l].astype(BF16), cw_f=ffn_conv_w[l], cb_f=ffn_conv_b[l].reshape(1, -1),
        wdn=w_down[l].astype(BF16),
    )


def _hyena_spectrum(L, p, tabs):
    zT, t_all = _filter_features(L)
    deltas = jnp.abs(jnp.linspace(HY_MIN_DECAY, HY_MAX_DECAY, HY_WIDTH, dtype=F32)).reshape(-1, 1)
    kT, sc = _filt(L, zT, t_all, p["w1T"], p["b1"], p["f1"], p["w2T"], p["b2"], p["f2"], p["w3T"], deltas)
    ncb = HY_WIDTH // LANES
    scale = sc.reshape(ncb, HY_ORDER, LANES).transpose(1, 0, 2).reshape(-1)
    return _kfft(scale, kT, tabs)


def _encoder_layer(x, p, tabs, final_g):
    B, S, _ = x.shape
    hyT, gT, q, k, v, o = _in_proj(x, p["g1"], p["whyT"], p["wqk"], p["wvo"], p["cw_qk"], p["cb_qk"], p["gb"])
    kr, ki = _hyena_spectrum(S, p, tabs)
    y_hy = _hyena(hyT, p["cw"], p["cbias"], p["hyb"], kr, ki, tabs)
    sf, mf, sb, mb = _mlstm_state(gT, k, v)
    y_ml = _mlstm_out(gT, q, k, v, o, sf, mf, sb, mb, p["ng"])
    xm = _out_proj(x, y_hy, y_ml, p["why_o"], p["wml_o"])
    gf = p["g2"] if final_g is None else final_g.reshape(1, D_MODEL)
    return _ffn(xm, p["g2"], p["wup"], p["cw_f"], p["cb_f"], p["wdn"], gf, final_g is not None)


def _trunk(x, layers, final_g):
    tabs = _dft_tables(x.shape[1])
    for l, p in enumerate(layers):
        x = _encoder_layer(x, p, tabs, final_g if l == len(layers) - 1 else None)
    return x


def kernel(x_prompt, x_sample, norm1_g, w_in, conv_w, conv_b, gate_b, filt_w1, filt_b1, filt_freq1, filt_w2,
           filt_b2, filt_freq2, filt_w3, hy_bias, ml_norm_g, w_out, norm2_g, w_up, ffn_conv_w, ffn_conv_b,
           w_down, final_g):
    weights = (norm1_g, w_in, conv_w, conv_b, gate_b, filt_w1, filt_b1, filt_freq1, filt_w2, filt_b2,
               filt_freq2, filt_w3, hy_bias, ml_norm_g, w_out, norm2_g, w_up, ffn_conv_w, ffn_conv_b, w_down)
    layers = [_layer_params(l, *weights) for l in range(norm1_g.shape[0])]
    return (_trunk(x_prompt, layers, final_g), _trunk(x_sample, layers, final_g))
```

```python
import functools
import math

import jax
import jax.numpy as jnp
from jax import lax
from jax.experimental import pallas as pl
from jax.experimental.pallas import tpu as pltpu

F32 = jnp.float32
BF16 = jnp.bfloat16

LANES = 128
SUBLANES = 8

D_MODEL = 1024
HY_WIDTH = 512
ML_WIDTH = 512
ML_HEADS = 4
ML_HEAD_DIM = ML_WIDTH // ML_HEADS
HY_ORDER = 2
HY_EMB = 33
HY_BANDS = (HY_EMB - 1) // 2
HY_HIDDEN = 64
HY_DECAY_TARGET = 1e-2
HY_MAX_DECAY = math.log(HY_DECAY_TARGET) / 0.3
HY_MIN_DECAY = math.log(HY_DECAY_TARGET) / 1.5
D_FF = 2816
EPS = 1e-6

HY_COLS = 3 * HY_WIDTH
OFF_Q = HY_COLS
OFF_K = OFF_Q + ML_WIDTH
OFF_V = OFF_K + ML_WIDTH
OFF_O = OFF_V + ML_WIDTH
OFF_G = OFF_O + ML_WIDTH
N_GATE = 4 * ML_HEADS
CONV_CH = OFF_V

ML_CHUNK = 256
GATE_W = 2048
IN_TM = 512
CONV_GW = 256
OUT_TM = 1024
FFN_TM = 512
FFN_FT = 256
FILT_PC = 1024
FFT_ROWS = 2048


def _rms(xv, g):
    ms = jnp.mean(xv * xv, axis=-1, keepdims=True)
    return xv * lax.rsqrt(ms + EPS) * g


def _sigmoid(x):
    return 1.0 / (1.0 + jnp.exp(-x))


def _conv3_rows(p, w, b, tm):
    n = p.shape[0]
    dn = pltpu.roll(p, 1, 0)[SUBLANES:tm + SUBLANES]
    up = pltpu.roll(p, n - 1, 0)[SUBLANES:tm + SUBLANES]
    mid = p[SUBLANES:tm + SUBLANES]
    return dn * w[0:1] + mid * w[1:2] + up * w[2:3] + b


def _in_proj_body(xp_ref, x_ref, xn_ref, g1_ref, wc_ref, wvo_ref, cw_ref, cb_ref, gb_ref,
                  hyT_ref, gT_ref, q_ref, k_ref, v_ref, o_ref, *, tm, cbn):
    i = pl.program_id(1)
    nt = pl.num_programs(1)
    g1 = g1_ref[...]
    h = _rms(x_ref[0], g1)
    hp = jnp.where(i > 0, _rms(xp_ref[0], g1), 0.0)
    hn = jnp.where(i < nt - 1, _rms(xn_ref[0], g1), 0.0)
    hext = jnp.concatenate([hp, h, hn], axis=0).astype(BF16)
    for g in range(CONV_CH // CONV_GW):
        cs = slice(g * CONV_GW, (g + 1) * CONV_GW)
        pc = jnp.dot(hext, wc_ref[:, cs], preferred_element_type=F32)
        c = _conv3_rows(pc, cw_ref[:, cs], cb_ref[:, cs], tm)
        if g < HY_COLS // CONV_GW:
            cT = c.T
            for j in range(tm // LANES):
                for blk in range(CONV_GW // cbn):
                    hyT_ref[0, g * (CONV_GW // cbn) + blk, j * cbn:(j + 1) * cbn, :] = (
                        cT[blk * cbn:(blk + 1) * cbn, j * LANES:(j + 1) * LANES])
        elif g < OFF_K // CONV_GW:
            q_ref[0, :, g * CONV_GW - OFF_Q:(g + 1) * CONV_GW - OFF_Q] = (c * _sigmoid(c)).astype(BF16)
        else:
            k_ref[0, :, g * CONV_GW - OFF_K:(g + 1) * CONV_GW - OFF_K] = (
                c * _sigmoid(c) * (ML_HEAD_DIM ** -0.5)).astype(BF16)
    pvo = jnp.dot(h.astype(BF16), wvo_ref[...], preferred_element_type=F32)
    v_ref[0] = pvo[:, :ML_WIDTH].astype(BF16)
    o_ref[0] = _sigmoid(pvo[:, ML_WIDTH:2 * ML_WIDTH])
    gT_ref[0] = pvo[:, 2 * ML_WIDTH:].T[:N_GATE, :] + gb_ref[...]


def _hyena_channels(S):
    return min(FFT_ROWS // (2 * S // LANES), 32)


def _in_proj(x, g1, wc, wvo, cw, cb, gb):
    B, S, D = x.shape
    cbn = _hyena_channels(S)
    tm = min(IN_TM, S)
    nt = S // tm
    r8 = tm // SUBLANES
    full = lambda shape: pl.BlockSpec(shape, lambda b, i: (0,) * len(shape))
    return pl.pallas_call(
        functools.partial(_in_proj_body, tm=tm, cbn=cbn),
        grid=(B, nt),
        in_specs=[
            pl.BlockSpec((1, SUBLANES, D), lambda b, i: (b, jnp.maximum(i * r8 - 1, 0), 0)),
            pl.BlockSpec((1, tm, D), lambda b, i: (b, i, 0)),
            pl.BlockSpec((1, SUBLANES, D), lambda b, i: (b, jnp.minimum((i + 1) * r8, S // SUBLANES - 1), 0)),
            full((1, D)), full(wc.shape), full(wvo.shape), full(cw.shape), full(cb.shape), full(gb.shape),
        ],
        out_specs=[
            pl.BlockSpec((1, HY_COLS // cbn, (tm // LANES) * cbn, LANES), lambda b, i: (b, 0, i, 0)),
            pl.BlockSpec((1, N_GATE, tm), lambda b, i: (b, 0, i)),
            pl.BlockSpec((1, tm, ML_WIDTH), lambda b, i: (b, i, 0)),
            pl.BlockSpec((1, tm, ML_WIDTH), lambda b, i: (b, i, 0)),
            pl.BlockSpec((1, tm, ML_WIDTH), lambda b, i: (b, i, 0)),
            pl.BlockSpec((1, tm, ML_WIDTH), lambda b, i: (b, i, 0)),
        ],
        out_shape=[
            jax.ShapeDtypeStruct((B, HY_COLS // cbn, (S // LANES) * cbn, LANES), F32),
            jax.ShapeDtypeStruct((B, N_GATE, S), F32),
            jax.ShapeDtypeStruct((B, S, ML_WIDTH), BF16),
            jax.ShapeDtypeStruct((B, S, ML_WIDTH), BF16),
            jax.ShapeDtypeStruct((B, S, ML_WIDTH), BF16),
            jax.ShapeDtypeStruct((B, S, ML_WIDTH), F32),
        ],
        name="in_proj",
    )(x, x, x, g1, wc, wvo, cw, cb, gb)


def _dft_tables(L):
    N = 2 * L
    N1 = N // LANES
    H1 = N1 // 2

    def cs(num, den):
        ang = (2.0 * math.pi / den) * (num % den).astype(F32)
        return jnp.cos(ang), jnp.sin(ang)

    k1 = jnp.arange(N1, dtype=jnp.int32)[:, None]
    c, s = cs(k1 * jnp.arange(H1, dtype=jnp.int32)[None, :], N1)
    a1 = jnp.block([[c, s], [-s, c]])
    a1i = jnp.block([[c.T, -s.T], [s.T, c.T]]) * (1.0 / N)
    cf, sf = cs(k1 * jnp.arange(N1, dtype=jnp.int32)[None, :], N1)
    a1k = jnp.concatenate([cf, -sf], axis=0)
    ct, st = cs(k1 * jnp.arange(LANES, dtype=jnp.int32)[None, :], N)
    twr, twi = ct, -st
    n2 = jnp.arange(LANES, dtype=jnp.int32)
    cg, sg = cs(n2[:, None] * n2[None, :], LANES)
    b2 = jnp.block([[cg, -sg], [sg, cg]])
    b2c = jnp.block([[cg, sg], [-sg, cg]])
    return dict(N1=N1, H1=H1, a1=a1, a1i=a1i, a1k=a1k, twr=twr, twi=twi, b2=b2, b2c=b2c)


def _filter_features(L):
    t = jnp.linspace(0.0, 1.0, L, dtype=F32)[:, None]
    freqs = jnp.linspace(1e-4, HY_BANDS - 1, HY_BANDS, dtype=F32)
    ang = (2.0 * math.pi / L) * jnp.arange(L, dtype=F32)[:, None] * freqs[None, :]
    z = jnp.concatenate([t, jnp.cos(ang), -jnp.sin(ang)], axis=-1)
    z_all = jnp.concatenate([z, z[0:1], z[L - 1:0:-1]], axis=0)
    t_all = jnp.concatenate([t, t[0:1], t[L - 1:0:-1]], axis=0)
    zT = jnp.pad(z_all, ((0, 0), (0, HY_HIDDEN - HY_EMB))).T
    return zT, t_all.T


def _split_bf16(x):
    hi = x.astype(BF16)
    return hi, (x - hi.astype(F32)).astype(BF16)


def _dot_split(w3, x):
    hi, lo = _split_bf16(x)
    return jnp.dot(w3, jnp.concatenate([hi, lo, hi], axis=0), preferred_element_type=F32)


def _split_cols(w):
    hi, lo = _split_bf16(w)
    return jnp.concatenate([hi, hi, lo], axis=-1)


def _filt_mlp_body(zT_ref, w1T_ref, b1_ref, f1_ref, w2T_ref, b2_ref, f2_ref, h2_ref):
    h1 = jnp.sin(f1_ref[...] * (_dot_split(w1T_ref[...], zT_ref[...]) + b1_ref[...]))
    h2_ref[...] = jnp.sin(f2_ref[...] * (_dot_split(w2T_ref[...], h1) + b2_ref[...]))


def _filt_body(h2_ref, t_ref, w3T_ref, dl_ref, kT_ref, sc_ref, *, L, pc):
    p = pl.program_id(1)
    last = pl.num_programs(1) - 1
    w3 = w3T_ref[0].reshape(HY_ORDER * LANES, 3 * HY_HIDDEN)
    h3 = _dot_split(w3, h2_ref[...])
    win = jnp.exp(-t_ref[...] * dl_ref[...])
    n_idx = p * pc + lax.broadcasted_iota(jnp.int32, (1, pc), 1)
    win = jnp.where(n_idx == L, 0.0, win)
    kern = h3 * jnp.concatenate([win] * HY_ORDER, axis=0)
    for o in range(HY_ORDER):
        for j in range(pc // LANES):
            kT_ref[o, j] = kern[o * LANES:(o + 1) * LANES, j * LANES:(j + 1) * LANES]
    ss = jnp.sum(kern * kern, axis=1, keepdims=True)

    @pl.when(p == 0)
    def _():
        sc_ref[0] = ss

    @pl.when(jnp.logical_and(p > 0, p < last))
    def _():
        sc_ref[0] = sc_ref[0] + ss

    @pl.when(p == last)
    def _():
        sc_ref[0] = lax.rsqrt(sc_ref[0] + ss)


def _filt(L, zT, t_all, w1T, b1, f1, w2T, b2, f2, w3T, dl):
    pc = min(FILT_PC, L)
    npc = 2 * L // pc
    ncb = HY_WIDTH // LANES
    full1 = lambda a: pl.BlockSpec(a.shape, lambda p: (0,) * a.ndim)
    h2T = pl.pallas_call(
        _filt_mlp_body,
        grid=(npc,),
        in_specs=[pl.BlockSpec((HY_HIDDEN, pc), lambda p: (0, p)),
                  full1(w1T), full1(b1), full1(f1), full1(w2T), full1(b2), full1(f2)],
        out_specs=pl.BlockSpec((HY_HIDDEN, pc), lambda p: (0, p)),
        out_shape=jax.ShapeDtypeStruct((HY_HIDDEN, 2 * L), F32),
        name="hyena_filter_mlp",
    )(zT, w1T, b1, f1, w2T, b2, f2)
    return pl.pallas_call(
        functools.partial(_filt_body, L=L, pc=pc),
        grid=(ncb, npc),
        in_specs=[
            pl.BlockSpec((HY_HIDDEN, pc), lambda c, p: (0, p)),
            pl.BlockSpec((1, pc), lambda c, p: (0, p)),
            pl.BlockSpec((1, HY_ORDER, LANES, 3 * HY_HIDDEN), lambda c, p: ((p * pc) // L, 0, c, 0)),
            pl.BlockSpec((LANES, 1), lambda c, p: (c, 0)),
        ],
        out_specs=[
            pl.BlockSpec((HY_ORDER, pc // LANES, LANES, LANES), lambda c, p: (0, p, c, 0)),
            pl.BlockSpec((1, HY_ORDER * LANES, 1), lambda c, p: (c, 0, 0)),
        ],
        out_shape=[
            jax.ShapeDtypeStruct((HY_ORDER, 2 * L // LANES, HY_WIDTH, LANES), F32),
            jax.ShapeDtypeStruct((ncb, HY_ORDER * LANES, 1), F32),
        ],
        name="hyena_filter",
    )(h2T, t_all, w3T, dl)


def _kfft_body(sc_ref, kT_ref, a1k_ref, twr_ref, twi_ref, b2_ref, kr_ref, ki_ref, x_s, l2_s, *, N1, cbn):
    o = pl.program_id(0)
    cb = pl.program_id(1)
    ks = jnp.swapaxes(kT_ref[...], 0, 1)
    for c in range(cbn):
        hi, lo = _split_bf16(ks[c] * sc_ref[o * HY_WIDTH + cb * cbn + c])
        cs = slice(c * LANES, (c + 1) * LANES)
        x_s[0:N1, cs] = hi
        x_s[N1:2 * N1, cs] = lo
        x_s[2 * N1:3 * N1, cs] = hi
    y = jnp.dot(a1k_ref[...], x_s[...], preferred_element_type=F32)
    twr = twr_ref[...]
    twi = twi_ref[...]
    for c in range(cbn):
        yr = y[0:N1, c * LANES:(c + 1) * LANES]
        yi = y[N1:2 * N1, c * LANES:(c + 1) * LANES]
        rs = slice(c * N1, (c + 1) * N1)
        for part, val in ((0, yr * twr - yi * twi), (1, yr * twi + yi * twr)):
            hi, lo = _split_bf16(val)
            l2_s[rs, part * LANES:(part + 1) * LANES] = hi
            l2_s[rs, (2 + part) * LANES:(3 + part) * LANES] = lo
            l2_s[rs, (4 + part) * LANES:(5 + part) * LANES] = hi
    z = jnp.dot(l2_s[...], b2_ref[...], preferred_element_type=F32)
    kr_ref[...] = z[:, :LANES]
    ki_ref[...] = z[:, LANES:]


def _kfft(scale, kT, tabs):
    N1 = tabs["N1"]
    cbn = _hyena_channels(N1 * LANES // 2)
    ncb = HY_WIDTH // cbn
    full = lambda a: pl.BlockSpec(a.shape, lambda o, c: (0,) * a.ndim)
    twr, twi = tabs["twr"], tabs["twi"]
    a_hi, a_lo = _split_bf16(tabs["a1k"])
    b_hi, b_lo = _split_bf16(tabs["b2"])
    a1k = jnp.concatenate([a_hi, a_hi, a_lo], axis=1)
    b2 = jnp.concatenate([b_hi, b_hi, b_lo], axis=0)
    return pl.pallas_call(
        functools.partial(_kfft_body, N1=N1, cbn=cbn),
        grid=(HY_ORDER, ncb),
        in_specs=[
            pl.BlockSpec(memory_space=pltpu.SMEM),
            pl.BlockSpec((None, N1, cbn, LANES), lambda o, c: (o, 0, c, 0)),
            full(a1k), full(twr), full(twi), full(b2),
        ],
        out_specs=[
            pl.BlockSpec((None, cbn * N1, LANES), lambda o, c: (o, c, 0)),
            pl.BlockSpec((None, cbn * N1, LANES), lambda o, c: (o, c, 0)),
        ],
        out_shape=[
            jax.ShapeDtypeStruct((HY_ORDER, HY_WIDTH * N1, LANES), F32),
            jax.ShapeDtypeStruct((HY_ORDER, HY_WIDTH * N1, LANES), F32),
        ],
        scratch_shapes=[pltpu.VMEM((3 * N1, cbn * LANES), BF16), pltpu.VMEM((cbn * N1, 6 * LANES), BF16)],
        name="hyena_kfft",
    )(scale, kT, a1k, twr, twi, b2)


def _hyena_body(hyb_ref, x1_ref, x2_ref, v_ref, kr_ref, ki_ref, a1_ref, twr_ref, twi_ref,
                b2_ref, b2c_ref, a1i_ref, o_ref, x_s, l2_s, r1_s, z_s, *, N1, H1, cbn):
    cb = pl.program_id(0)

    def slabs(ref, e):
        return jnp.swapaxes(ref[e].reshape(H1, cbn, LANES), 0, 1)

    def long_conv(order):
        twr = twr_ref[...]
        twi = twi_ref[...]
        y = jnp.dot(a1_ref[...], x_s[...], preferred_element_type=F32)
        for c in range(cbn):
            yr = y[0:N1, c * LANES:(c + 1) * LANES]
            yi = y[N1:2 * N1, c * LANES:(c + 1) * LANES]
            l2_s[c * N1:(c + 1) * N1, 0:LANES] = (yr * twr - yi * twi).astype(BF16)
            l2_s[c * N1:(c + 1) * N1, LANES:2 * LANES] = (yr * twi + yi * twr).astype(BF16)
        z = jnp.dot(l2_s[...], b2_ref[...], preferred_element_type=F32)
        zr = z[:, :LANES]
        zi = z[:, LANES:]
        kr = kr_ref[order]
        ki = ki_ref[order]
        l2_s[:, 0:LANES] = (zr * kr - zi * ki).astype(BF16)
        l2_s[:, LANES:2 * LANES] = (zr * ki + zi * kr).astype(BF16)
        v = jnp.dot(l2_s[...], b2c_ref[...], preferred_element_type=F32)
        for c in range(cbn):
            vr = v[c * N1:(c + 1) * N1, :LANES]
            vi = v[c * N1:(c + 1) * N1, LANES:]
            r1_s[0:N1, c * LANES:(c + 1) * LANES] = (vr * twr + vi * twi).astype(BF16)
            r1_s[N1:2 * N1, c * LANES:(c + 1) * LANES] = (vi * twr - vr * twi).astype(BF16)
        return jnp.dot(a1i_ref[...], r1_s[...], preferred_element_type=F32)

    for e in range(2):
        vs = slabs(v_ref, e)
        z_s[e] = vs
        for c in range(cbn):
            x_s[e * H1:(e + 1) * H1, c * LANES:(c + 1) * LANES] = vs[c].astype(BF16)
    y1 = long_conv(0)
    for e in range(2):
        x1s = slabs(x1_ref, e)
        for c in range(cbn):
            c1 = y1[e * H1:(e + 1) * H1, c * LANES:(c + 1) * LANES] + hyb_ref[cb * cbn + c] * z_s[e, c]
            z2 = x1s[c] * c1
            z_s[e, c] = z2
            x_s[e * H1:(e + 1) * H1, c * LANES:(c + 1) * LANES] = z2.astype(BF16)
    y2 = long_conv(1)
    for e in range(2):
        x2s = slabs(x2_ref, e)
        for c in range(cbn):
            c2 = (y2[e * H1:(e + 1) * H1, c * LANES:(c + 1) * LANES]
                  + hyb_ref[HY_WIDTH + cb * cbn + c] * z_s[e, c])
            o_ref[e, c] = x2s[c] * c2


def _hyena(hyT, hyb, kr, ki, tabs):
    B = hyT.shape[0]
    N1 = tabs["N1"]
    H1 = N1 // 2
    cbn = _hyena_channels(H1 * LANES)
    assert hyT.shape == (B, HY_COLS // cbn, H1 * cbn, LANES) and B % 2 == 0
    ncb = HY_WIDTH // cbn
    hy5 = hyT.reshape(B // 2, 2, HY_COLS // cbn, H1 * cbn, LANES)
    a1 = tabs["a1"].astype(BF16)
    a1i = tabs["a1i"].astype(BF16)
    b2 = tabs["b2"].astype(BF16)
    b2c = tabs["b2c"].astype(BF16)
    twr, twi = tabs["twr"], tabs["twi"]
    full = lambda a: pl.BlockSpec(a.shape, lambda c, p: (0,) * a.ndim)
    smem = pl.BlockSpec(memory_space=pltpu.SMEM)
    xspec = lambda k: pl.BlockSpec((None, 2, None, H1 * cbn, LANES), lambda c, p: (p, 0, k * ncb + c, 0, 0))
    kspec = pl.BlockSpec((HY_ORDER, cbn * N1, LANES), lambda c, p: (0, c, 0))
    out = pl.pallas_call(
        functools.partial(_hyena_body, N1=N1, H1=H1, cbn=cbn),
        grid=(ncb, B // 2),
        in_specs=[smem, xspec(0), xspec(1), xspec(2), kspec, kspec,
                  full(a1), full(twr), full(twi), full(b2), full(b2c), full(a1i)],
        out_specs=pl.BlockSpec((None, 2, cbn, H1, LANES), lambda c, p: (p, 0, c, 0, 0)),
        out_shape=jax.ShapeDtypeStruct((B // 2, 2, HY_WIDTH, H1, LANES), F32),
        scratch_shapes=[
            pltpu.VMEM((2 * H1, cbn * LANES), BF16),
            pltpu.VMEM((cbn * N1, 2 * LANES), BF16),
            pltpu.VMEM((2 * N1, cbn * LANES), BF16),
            pltpu.VMEM((2, cbn, H1, LANES), F32),
        ],
        name="hyena_mix",
    )(hyb, hy5, hy5, hy5, kr, ki, a1, twr, twi, b2, b2c, a1i)
    return out.reshape(B, HY_WIDTH, H1, LANES)


GP_ROWS = 6 * ML_HEADS


def _seg_scan(x, op, reverse, fill, T):
    W = x.shape[1]
    pos = lax.broadcasted_iota(jnp.int32, x.shape, 1) % T
    k = 1
    while k < T:
        if reverse:
            x = op(x, jnp.where(pos < T - k, pltpu.roll(x, W - k, 1), fill))
        else:
            x = op(x, jnp.where(pos >= k, pltpu.roll(x, k, 1), fill))
        k *= 2
    return x


def _log_sigmoid(x):
    return jnp.minimum(x, 0.0) - jnp.log(1.0 + jnp.exp(-jnp.abs(x)))


def _gate_prep_body(g_ref, o_ref, *, T):
    g = g_ref[0]
    for d in range(2):
        ig = g[8 * d:8 * d + ML_HEADS]
        fg = g[8 * d + ML_HEADS:8 * d + 2 * ML_HEADS]
        b = _seg_scan(_log_sigmoid(fg), jnp.add, d == 1, 0.0, T)
        a = ig - b
        o_ref[0, 4 * d:4 * d + 4, :] = a
        o_ref[0, 8 + 4 * d:12 + 4 * d, :] = b
        o_ref[0, 16 + 4 * d:20 + 4 * d, :] = _seg_scan(a, jnp.maximum, d == 1, -jnp.inf, T)


def _gate_prep(gT, T):
    B, _, S = gT.shape
    w = min(GATE_W, S)
    return pl.pallas_call(
        functools.partial(_gate_prep_body, T=T),
        grid=(B, S // w),
        in_specs=[pl.BlockSpec((1, N_GATE, w), lambda b, i: (b, 0, i))],
        out_specs=pl.BlockSpec((1, GP_ROWS, w), lambda b, i: (b, 0, i)),
        out_shape=jax.ShapeDtypeStruct((B, GP_ROWS, S), F32),
        name="mlstm_gates",
    )(gT)


def _gp_dir(gp, d, T):
    a = gp[4 * d:4 * d + 4]
    b = gp[8 + 4 * d:12 + 4 * d]
    cm = gp[16 + 4 * d:20 + 4 * d]
    e = 0 if d == 1 else T - 1
    return a, b, cm, b[:, e:e + 1], cm[:, e:e + 1]


def _ones_col(T):
    return (lax.broadcasted_iota(jnp.int32, (T, LANES), 1) == 0).astype(BF16)


def _mlstm_state_body(gf_ref, gb_ref, kf_ref, vf_ref, kb_ref, vb_ref, sf_ref, mf_ref, sb_ref, mb_ref,
                      s_s, m_s, *, T):
    i = pl.program_id(1)

    @pl.when(i == 0)
    def _():
        s_s[...] = jnp.zeros(s_s.shape, F32)
        m_s[...] = jnp.zeros(m_s.shape, F32)

    ones = _ones_col(T)
    for d, (g_ref, k_ref, v_ref, s_out, m_out) in enumerate(
            ((gf_ref, kf_ref, vf_ref, sf_ref, mf_ref), (gb_ref, kb_ref, vb_ref, sb_ref, mb_ref))):
        a, _, _, btot, amax = _gp_dir(g_ref[0], d, T)
        w = jnp.exp(a - amax)
        m_old = m_s[d, 0:ML_HEADS, 0:1]
        m_out[0, 0] = m_s[d]
        m_new = jnp.maximum(btot + m_old, btot + amax)
        s_old = jnp.exp(btot + m_old - m_new)
        s_loc = jnp.exp(btot + amax - m_new)
        m_s[d, 0:ML_HEADS, :] = jnp.broadcast_to(m_new, (ML_HEADS, LANES))
        for h in range(ML_HEADS):
            hs = slice(h * ML_HEAD_DIM, (h + 1) * ML_HEAD_DIM)
            kT = k_ref[0, :, hs].astype(F32).T
            kwT = (kT * w[h:h + 1, :]).astype(BF16)
            v1 = jnp.concatenate([v_ref[0, :, hs], ones], axis=1)
            c_loc = jnp.dot(kwT, v1, preferred_element_type=F32)
            st = s_s[d, h]
            s_out[0, 0, h] = st.astype(BF16)
            s_s[d, h] = s_old[h:h + 1, :] * st + s_loc[h:h + 1, :] * c_loc


def _mlstm_state(gp, k, v, T):
    B, S, _ = k.shape
    nc = S // T
    sd = 2 * ML_HEAD_DIM
    gspec = lambda rev: pl.BlockSpec((1, GP_ROWS, T), (lambda b, i: (b, 0, nc - 1 - i)) if rev else (lambda b, i: (b, 0, i)))
    xspec = lambda rev: pl.BlockSpec((1, T, ML_WIDTH), (lambda b, i: (b, nc - 1 - i, 0)) if rev else (lambda b, i: (b, i, 0)))
    sspec = lambda rev: pl.BlockSpec((1, 1, ML_HEADS, ML_HEAD_DIM, sd),
                                     (lambda b, i: (b, nc - 1 - i, 0, 0, 0)) if rev else (lambda b, i: (b, i, 0, 0, 0)))
    mspec = lambda rev: pl.BlockSpec((1, 1, SUBLANES, LANES),
                                     (lambda b, i: (b, nc - 1 - i, 0, 0)) if rev else (lambda b, i: (b, i, 0, 0)))
    return pl.pallas_call(
        functools.partial(_mlstm_state_body, T=T),
        grid=(B, nc),
        in_specs=[gspec(False), gspec(True), xspec(False), xspec(False), xspec(True), xspec(True)],
        out_specs=[sspec(False), mspec(False), sspec(True), mspec(True)],
        out_shape=[
            jax.ShapeDtypeStruct((B, nc, ML_HEADS, ML_HEAD_DIM, sd), BF16),
            jax.ShapeDtypeStruct((B, nc, SUBLANES, LANES), F32),
            jax.ShapeDtypeStruct((B, nc, ML_HEADS, ML_HEAD_DIM, sd), BF16),
            jax.ShapeDtypeStruct((B, nc, SUBLANES, LANES), F32),
        ],
        scratch_shapes=[pltpu.VMEM((2, ML_HEADS, ML_HEAD_DIM, sd), F32), pltpu.VMEM((2, SUBLANES, LANES), F32)],
        name="mlstm_state",
    )(gp, gp, k, v, k, v)


def _mlstm_out_body(g_ref, q_ref, k_ref, v_ref, o_ref, sf_ref, mf_ref, sb_ref, mb_ref, ng_ref, y_ref, *, T):
    gp = g_ref[0]
    a_rows = []
    cols = []
    for d, m_ref in enumerate((mf_ref, mb_ref)):
        a, b, cm, _, _ = _gp_dir(gp, d, T)
        m = m_ref[0, 0][0:ML_HEADS, 0:1]
        mt = jnp.maximum(cm, m)
        a_rows.append(a)
        cols.append((mt, jnp.exp(-(b + mt)), jnp.exp(m - mt)))
    stacked = jnp.concatenate(
        [cols[0][0], cols[1][0], cols[0][1], cols[1][1], cols[0][2], cols[1][2],
         jnp.zeros((LANES - 6 * ML_HEADS, T), F32)], axis=0)
    colsT = stacked.T
    tt = lax.broadcasted_iota(jnp.int32, (T, T), 0)
    ss = lax.broadcasted_iota(jnp.int32, (T, T), 1)
    masks = (ss <= tt, ss >= tt)
    ones = _ones_col(T)
    hsl = [slice(h * ML_HEAD_DIM, (h + 1) * ML_HEAD_DIM) for h in range(ML_HEADS)]
    ps, qss, v1s = [], [], []
    for h in range(ML_HEADS):
        qh = q_ref[0, :, hsl[h]]
        ps.append(lax.dot_general(qh, k_ref[0, :, hsl[h]], (((1,), (1,)), ((), ())),
                                  preferred_element_type=F32))
        s_fb = jnp.concatenate([sf_ref[0, 0, h], sb_ref[0, 0, h]], axis=1)
        qss.append(jnp.dot(qh, s_fb, preferred_element_type=F32))
        v1s.append(jnp.concatenate([v_ref[0, :, hsl[h]], ones], axis=1))
    sms = []
    for h in range(ML_HEADS):
        for d in range(2):
            mt_c = colsT[:, 4 * d + h:4 * d + h + 1]
            e = jnp.exp(jnp.where(masks[d], a_rows[d][h:h + 1, :] - mt_c, -jnp.inf))
            sms.append((ps[h] * e).astype(BF16))
    accs = [jnp.dot(sms[2 * h + d], v1s[h], preferred_element_type=F32)
            for h in range(ML_HEADS) for d in range(2)]
    for h in range(ML_HEADS):
        hm = None
        for d in range(2):
            emt_c = colsT[:, 8 + 4 * d + h:8 + 4 * d + h + 1]
            int_c = colsT[:, 16 + 4 * d + h:16 + 4 * d + h + 1]
            acc = accs[2 * h + d] + int_c * qss[h][:, 2 * ML_HEAD_DIM * d:2 * ML_HEAD_DIM * (d + 1)]
            num = acc[:, :ML_HEAD_DIM]
            den = acc[:, ML_HEAD_DIM:ML_HEAD_DIM + 1]
            hd = num / jnp.maximum(jnp.abs(den), emt_c)
            hm = hd if hm is None else hm + hd
        hn = hm * lax.rsqrt(jnp.mean(hm * hm, axis=-1, keepdims=True) + EPS)
        y_ref[0, :, hsl[h]] = hn * ng_ref[:, hsl[h]] * o_ref[0, :, hsl[h]]


def _mlstm_out(gp, q, k, v, o, sf, mf, sb, mb, ng, T):
    B, S, _ = k.shape
    nc = S // T
    sd = 2 * ML_HEAD_DIM
    xspec = pl.BlockSpec((1, T, ML_WIDTH), lambda b, i: (b, i, 0))
    sspec = pl.BlockSpec((1, 1, ML_HEADS, ML_HEAD_DIM, sd), lambda b, i: (b, i, 0, 0, 0))
    mspec = pl.BlockSpec((1, 1, SUBLANES, LANES), lambda b, i: (b, i, 0, 0))
    return pl.pallas_call(
        functools.partial(_mlstm_out_body, T=T),
        grid=(B, nc),
        in_specs=[pl.BlockSpec((1, GP_ROWS, T), lambda b, i: (b, 0, i)), xspec, xspec, xspec, xspec,
                  sspec, mspec, sspec, mspec, pl.BlockSpec((1, ML_WIDTH), lambda b, i: (0, 0))],
        out_specs=xspec,
        out_shape=jax.ShapeDtypeStruct((B, S, ML_WIDTH), F32),
        name="mlstm_out",
    )(gp, q, k, v, o, sf, mf, sb, mb, ng)


def _out_proj_body(x_ref, yhy_ref, yml_ref, why_ref, wml_ref, o_ref, *, tm):
    yt = jnp.concatenate([yhy_ref[0, :, j, :].T for j in range(tm // LANES)], axis=0)
    acc = jnp.dot(yt.astype(BF16), why_ref[...], preferred_element_type=F32)
    acc = acc + jnp.dot(yml_ref[0].astype(BF16), wml_ref[...], preferred_element_type=F32)
    o_ref[0] = x_ref[0] + acc


def _out_proj(x, yhy, yml, why, wml):
    B, S, D = x.shape
    tm = min(OUT_TM, S)
    full = lambda a: pl.BlockSpec(a.shape, lambda b, i: (0,) * a.ndim)
    return pl.pallas_call(
        functools.partial(_out_proj_body, tm=tm),
        grid=(B, S // tm),
        in_specs=[
            pl.BlockSpec((1, tm, D), lambda b, i: (b, i, 0)),
            pl.BlockSpec((1, HY_WIDTH, tm // LANES, LANES), lambda b, i: (b, 0, i, 0)),
            pl.BlockSpec((1, tm, ML_WIDTH), lambda b, i: (b, i, 0)),
            full(why), full(wml),
        ],
        out_specs=pl.BlockSpec((1, tm, D), lambda b, i: (b, i, 0)),
        out_shape=jax.ShapeDtypeStruct((B, S, D), F32),
        name="out_proj",
    )(x, yhy, yml, why, wml)


def _ffn_body(xp_ref, x_ref, xn_ref, g2_ref, wup_ref, cw_ref, cb_ref, wdn_ref, gf_ref, o_ref, act_s,
              *, tm, ft, final_norm):
    i = pl.program_id(1)
    nt = pl.num_programs(1)
    g2 = g2_ref[...]
    xm = x_ref[0]
    h = _rms(xm, g2)
    hp = jnp.where(i > 0, _rms(xp_ref[0], g2), 0.0)
    hn = jnp.where(i < nt - 1, _rms(xn_ref[0], g2), 0.0)
    hext = jnp.concatenate([hp, h, hn], axis=0).astype(BF16)
    for j in range(D_FF // ft):
        cv = slice(j * ft, (j + 1) * ft)
        cg = slice(D_FF + j * ft, D_FF + (j + 1) * ft)
        uv = jnp.dot(hext, wup_ref[:, cv], preferred_element_type=F32)
        ug = jnp.dot(hext, wup_ref[:, cg], preferred_element_type=F32)
        val = _conv3_rows(uv, cw_ref[:, cv], cb_ref[:, cv], tm)
        gate = _conv3_rows(ug, cw_ref[:, cg], cb_ref[:, cg], tm)
        act_s[:, cv] = (gate * _sigmoid(gate) * val).astype(BF16)
    xo = xm + jnp.dot(act_s[...], wdn_ref[...], preferred_element_type=F32)
    o_ref[0] = _rms(xo, gf_ref[...]) if final_norm else xo


def _ffn(x, g2, wup, cw, cb, wdn, gf, final_norm):
    B, S, D = x.shape
    tm = min(FFN_TM, S)
    nt = S // tm
    r8 = tm // SUBLANES
    full = lambda a: pl.BlockSpec(a.shape, lambda b, i: (0,) * a.ndim)
    once = lambda a: pl.BlockSpec(a.shape, lambda b, i: (0,) * a.ndim, pipeline_mode=pl.Buffered(1))
    return pl.pallas_call(
        functools.partial(_ffn_body, tm=tm, ft=FFN_FT, final_norm=final_norm),
        grid=(B, nt),
        in_specs=[
            pl.BlockSpec((1, SUBLANES, D), lambda b, i: (b, jnp.maximum(i * r8 - 1, 0), 0)),
            pl.BlockSpec((1, tm, D), lambda b, i: (b, i, 0)),
            pl.BlockSpec((1, SUBLANES, D), lambda b, i: (b, jnp.minimum((i + 1) * r8, S // SUBLANES - 1), 0)),
            full(g2), once(wup), full(cw), full(cb), once(wdn), full(gf),
        ],
        out_specs=pl.BlockSpec((1, tm, D), lambda b, i: (b, i, 0)),
        out_shape=jax.ShapeDtypeStruct((B, S, D), F32),
        scratch_shapes=[pltpu.VMEM((tm, D_FF), BF16)],
        name="ffn",
    )(x, x, x, g2, wup, cw, cb, wdn, gf)


def _layer_params(l, norm1_g, w_in, conv_w, conv_b, gate_b, filt_w1, filt_b1, filt_freq1, filt_w2, filt_b2,
                  filt_freq2, filt_w3, hy_bias, ml_norm_g, w_out, norm2_g, w_up, ffn_conv_w, ffn_conv_b, w_down):
    w = w_in[l]
    col = lambda a: a.astype(F32).reshape(-1, 1)
    return dict(
        g1=norm1_g[l].reshape(1, D_MODEL),
        wc=w[:, :CONV_CH].astype(BF16),
        wvo=jnp.pad(w[:, OFF_V:], ((0, 0), (0, LANES - N_GATE))).astype(BF16),
        cw=conv_w[l], cb=conv_b[l].reshape(1, -1),
        gb=col(gate_b[l]),
        hyb=hy_bias[l].reshape(-1),
        w1T=_split_cols(jnp.pad(filt_w1[l], ((0, HY_HIDDEN - HY_EMB), (0, 0))).T),
        b1=col(filt_b1[l]), f1=col(filt_freq1[l]),
        w2T=_split_cols(filt_w2[l].T), b2=col(filt_b2[l]), f2=col(filt_freq2[l]),
        w3T=_split_cols(filt_w3[l].T).reshape(2, HY_ORDER, HY_WIDTH, 3 * HY_HIDDEN),
        ng=ml_norm_g[l].reshape(1, ML_WIDTH),
        why_o=w_out[l][:HY_WIDTH].astype(BF16), wml_o=w_out[l][HY_WIDTH:].astype(BF16),
        g2=norm2_g[l].reshape(1, D_MODEL),
        wup=w_up[l].astype(BF16), cw_f=ffn_conv_w[l], cb_f=ffn_conv_b[l].reshape(1, -1),
        wdn=w_down[l].astype(BF16),
    )


def _hyena_spectrum(L, p, tabs):
    zT, t_all = _filter_features(L)
    deltas = jnp.abs(jnp.linspace(HY_MIN_DECAY, HY_MAX_DECAY, HY_WIDTH, dtype=F32)).reshape(-1, 1)
    kT, sc = _filt(L, zT, t_all, p["w1T"], p["b1"], p["f1"], p["w2T"], p["b2"], p["f2"], p["w3T"], deltas)
    ncb = HY_WIDTH // LANES
    scale = sc.reshape(ncb, HY_ORDER, LANES).transpose(1, 0, 2).reshape(-1)
    return _kfft(scale, kT, tabs)


def _encoder_layer(x, p, tabs, final_g):
    B, S, _ = x.shape
    hyT, gT, q, k, v, o = _in_proj(x, p["g1"], p["wc"], p["wvo"], p["cw"], p["cb"], p["gb"])
    kr, ki = _hyena_spectrum(S, p, tabs)
    y_hy = _hyena(hyT, p["hyb"], kr, ki, tabs)
    T = min(ML_CHUNK, S)
    gp = _gate_prep(gT, T)
    sf, mf, sb, mb = _mlstm_state(gp, k, v, T)
    y_ml = _mlstm_out(gp, q, k, v, o, sf, mf, sb, mb, p["ng"], T)
    xm = _out_proj(x, y_hy, y_ml, p["why_o"], p["wml_o"])
    gf = p["g2"] if final_g is None else final_g.reshape(1, D_MODEL)
    return _ffn(xm, p["g2"], p["wup"], p["cw_f"], p["cb_f"], p["wdn"], gf, final_g is not None)


def _trunk(x, layers, final_g):
    tabs = _dft_tables(x.shape[1])
    for l, p in enumerate(layers):
        x = _encoder_layer(x, p, tabs, final_g if l == len(layers) - 1 else None)
    return x


def kernel(x_prompt, x_sample, norm1_g, w_in, conv_w, conv_b, gate_b, filt_w1, filt_b1, filt_freq1, filt_w2,
           filt_b2, filt_freq2, filt_w3, hy_bias, ml_norm_g, w_out, norm2_g, w_up, ffn_conv_w, ffn_conv_b,
           w_down, final_g):
    weights = (norm1_g, w_in, conv_w, conv_b, gate_b, filt_w1, filt_b1, filt_freq1, filt_w2, filt_b2,
               filt_freq2, filt_w3, hy_bias, ml_norm_g, w_out, norm2_g, w_up, ffn_conv_w, ffn_conv_b, w_down)
    layers = [_layer_params(l, *weights) for l in range(norm1_g.shape[0])]
    return (_trunk(x_prompt, layers, final_g), _trunk(x_sample, layers, final_g))
```

```python
import functools
import math

import jax
import jax.numpy as jnp
from jax import lax
from jax.experimental import pallas as pl
from jax.experimental.pallas import tpu as pltpu

F32 = jnp.float32
BF16 = jnp.bfloat16

LANES = 128
SUBLANES = 8

D_MODEL = 1024
HY_WIDTH = 512
ML_WIDTH = 512
ML_HEADS = 4
ML_HEAD_DIM = ML_WIDTH // ML_HEADS
HY_ORDER = 2
HY_EMB = 33
HY_BANDS = (HY_EMB - 1) // 2
HY_HIDDEN = 64
HY_DECAY_TARGET = 1e-2
HY_MAX_DECAY = math.log(HY_DECAY_TARGET) / 0.3
HY_MIN_DECAY = math.log(HY_DECAY_TARGET) / 1.5
D_FF = 2816
EPS = 1e-6

HY_COLS = 3 * HY_WIDTH
OFF_Q = HY_COLS
OFF_K = OFF_Q + ML_WIDTH
OFF_V = OFF_K + ML_WIDTH
OFF_O = OFF_V + ML_WIDTH
OFF_G = OFF_O + ML_WIDTH
N_GATE = 4 * ML_HEADS
CONV_CH = OFF_V

ML_CHUNK = 256
ML_OUT_CHUNKS = 4
GATE_W = 2048
IN_TM = 512
CONV_GW = 256
OUT_TM = 1024
FFN_TM = 512
FFN_FT = 256
FILT_PC = 1024
FFT_ROWS = 2048


def _rms(xv, g):
    ms = jnp.mean(xv * xv, axis=-1, keepdims=True)
    return xv * lax.rsqrt(ms + EPS) * g


def _sigmoid(x):
    return 1.0 / (1.0 + jnp.exp(-x))


def _conv3_rows(p, w, b, tm):
    n = p.shape[0]
    dn = pltpu.roll(p, 1, 0)[SUBLANES:tm + SUBLANES]
    up = pltpu.roll(p, n - 1, 0)[SUBLANES:tm + SUBLANES]
    mid = p[SUBLANES:tm + SUBLANES]
    return dn * w[0:1] + mid * w[1:2] + up * w[2:3] + b


def _in_proj_body(xp_ref, x_ref, xn_ref, g1_ref, wc_ref, wvo_ref, cw_ref, cb_ref, gb_ref,
                  hyT_ref, gT_ref, q_ref, k_ref, v_ref, o_ref, *, tm, cbn):
    i = pl.program_id(1)
    nt = pl.num_programs(1)
    g1 = g1_ref[...]
    h = _rms(x_ref[0], g1)
    hp = jnp.where(i > 0, _rms(xp_ref[0], g1), 0.0)
    hn = jnp.where(i < nt - 1, _rms(xn_ref[0], g1), 0.0)
    hext = jnp.concatenate([hp, h, hn], axis=0).astype(BF16)
    for g in range(CONV_CH // CONV_GW):
        cs = slice(g * CONV_GW, (g + 1) * CONV_GW)
        pc = jnp.dot(hext, wc_ref[:, cs], preferred_element_type=F32)
        c = _conv3_rows(pc, cw_ref[:, cs], cb_ref[:, cs], tm)
        if g < HY_COLS // CONV_GW:
            cT = c.T
            for j in range(tm // LANES):
                for blk in range(CONV_GW // cbn):
                    hyT_ref[0, g * (CONV_GW // cbn) + blk, j * cbn:(j + 1) * cbn, :] = (
                        cT[blk * cbn:(blk + 1) * cbn, j * LANES:(j + 1) * LANES])
        elif g < OFF_K // CONV_GW:
            q_ref[0, g * CONV_GW - OFF_Q:(g + 1) * CONV_GW - OFF_Q, :] = (c * _sigmoid(c)).T.astype(BF16)
        else:
            k_ref[0, :, g * CONV_GW - OFF_K:(g + 1) * CONV_GW - OFF_K] = (
                c * _sigmoid(c) * (ML_HEAD_DIM ** -0.5)).astype(BF16)
    rT = lax.dot_general(wvo_ref[...], h.astype(BF16), (((1,), (1,)), ((), ())),
                         preferred_element_type=F32)
    v_ref[0] = rT[:ML_WIDTH].astype(BF16)
    o_ref[0] = _sigmoid(rT[ML_WIDTH:2 * ML_WIDTH])
    gT_ref[0] = rT[2 * ML_WIDTH:] + gb_ref[...]


def _hyena_channels(S):
    return min(FFT_ROWS // (2 * S // LANES), 32)


def _in_proj(x, g1, wc, wvo, cw, cb, gb):
    B, S, D = x.shape
    cbn = _hyena_channels(S)
    tm = min(IN_TM, S)
    nt = S // tm
    r8 = tm // SUBLANES
    full = lambda shape: pl.BlockSpec(shape, lambda b, i: (0,) * len(shape))
    return pl.pallas_call(
        functools.partial(_in_proj_body, tm=tm, cbn=cbn),
        grid=(B, nt),
        in_specs=[
            pl.BlockSpec((1, SUBLANES, D), lambda b, i: (b, jnp.maximum(i * r8 - 1, 0), 0)),
            pl.BlockSpec((1, tm, D), lambda b, i: (b, i, 0)),
            pl.BlockSpec((1, SUBLANES, D), lambda b, i: (b, jnp.minimum((i + 1) * r8, S // SUBLANES - 1), 0)),
            full((1, D)), full(wc.shape), full(wvo.shape), full(cw.shape), full(cb.shape), full(gb.shape),
        ],
        out_specs=[
            pl.BlockSpec((1, HY_COLS // cbn, (tm // LANES) * cbn, LANES), lambda b, i: (b, 0, i, 0)),
            pl.BlockSpec((1, N_GATE, tm), lambda b, i: (b, 0, i)),
            pl.BlockSpec((1, ML_WIDTH, tm), lambda b, i: (b, 0, i)),
            pl.BlockSpec((1, tm, ML_WIDTH), lambda b, i: (b, i, 0)),
            pl.BlockSpec((1, ML_WIDTH, tm), lambda b, i: (b, 0, i)),
            pl.BlockSpec((1, ML_WIDTH, tm), lambda b, i: (b, 0, i)),
        ],
        out_shape=[
            jax.ShapeDtypeStruct((B, HY_COLS // cbn, (S // LANES) * cbn, LANES), F32),
            jax.ShapeDtypeStruct((B, N_GATE, S), F32),
            jax.ShapeDtypeStruct((B, ML_WIDTH, S), BF16),
            jax.ShapeDtypeStruct((B, S, ML_WIDTH), BF16),
            jax.ShapeDtypeStruct((B, ML_WIDTH, S), BF16),
            jax.ShapeDtypeStruct((B, ML_WIDTH, S), F32),
        ],
        name="in_proj",
    )(x, x, x, g1, wc, wvo, cw, cb, gb)


def _dft_tables(L):
    N = 2 * L
    N1 = N // LANES
    H1 = N1 // 2

    def cs(num, den):
        ang = (2.0 * math.pi / den) * (num % den).astype(F32)
        return jnp.cos(ang), jnp.sin(ang)

    k1 = jnp.arange(N1, dtype=jnp.int32)[:, None]
    c, s = cs(k1 * jnp.arange(H1, dtype=jnp.int32)[None, :], N1)
    a1 = jnp.block([[c, s], [-s, c]])
    a1i = jnp.block([[c.T, -s.T], [s.T, c.T]]) * (1.0 / N)
    cf, sf = cs(k1 * jnp.arange(N1, dtype=jnp.int32)[None, :], N1)
    a1k = jnp.concatenate([cf, -sf], axis=0)
    ct, st = cs(k1 * jnp.arange(LANES, dtype=jnp.int32)[None, :], N)
    twr, twi = ct, -st
    n2 = jnp.arange(LANES, dtype=jnp.int32)
    cg, sg = cs(n2[:, None] * n2[None, :], LANES)
    b2 = jnp.block([[cg, -sg], [sg, cg]])
    b2c = jnp.block([[cg, sg], [-sg, cg]])
    return dict(N1=N1, H1=H1, a1=a1, a1i=a1i, a1k=a1k, twr=twr, twi=twi, b2=b2, b2c=b2c)


def _filter_features(L):
    t = jnp.linspace(0.0, 1.0, L, dtype=F32)[:, None]
    freqs = jnp.linspace(1e-4, HY_BANDS - 1, HY_BANDS, dtype=F32)
    ang = (2.0 * math.pi / L) * jnp.arange(L, dtype=F32)[:, None] * freqs[None, :]
    z = jnp.concatenate([t, jnp.cos(ang), -jnp.sin(ang)], axis=-1)
    z_all = jnp.concatenate([z, z[0:1], z[L - 1:0:-1]], axis=0)
    t_all = jnp.concatenate([t, t[0:1], t[L - 1:0:-1]], axis=0)
    zT = jnp.pad(z_all, ((0, 0), (0, HY_HIDDEN - HY_EMB))).T
    return zT, t_all.T


def _split_bf16(x):
    hi = x.astype(BF16)
    return hi, (x - hi.astype(F32)).astype(BF16)


def _dot_split(w3, x):
    hi, lo = _split_bf16(x)
    return jnp.dot(w3, jnp.concatenate([hi, lo, hi], axis=0), preferred_element_type=F32)


def _split_cols(w):
    hi, lo = _split_bf16(w)
    return jnp.concatenate([hi, hi, lo], axis=-1)


def _filt_mlp_body(zT_ref, w1T_ref, b1_ref, f1_ref, w2T_ref, b2_ref, f2_ref, h2_ref):
    h1 = jnp.sin(f1_ref[...] * (_dot_split(w1T_ref[...], zT_ref[...]) + b1_ref[...]))
    h2_ref[...] = jnp.sin(f2_ref[...] * (_dot_split(w2T_ref[...], h1) + b2_ref[...]))


def _filt_body(h2_ref, t_ref, w3T_ref, dl_ref, kT_ref, sc_ref, *, L, pc):
    p = pl.program_id(1)
    last = pl.num_programs(1) - 1
    w3 = w3T_ref[0].reshape(HY_ORDER * LANES, 3 * HY_HIDDEN)
    h3 = _dot_split(w3, h2_ref[...])
    win = jnp.exp(-t_ref[...] * dl_ref[...])
    n_idx = p * pc + lax.broadcasted_iota(jnp.int32, (1, pc), 1)
    win = jnp.where(n_idx == L, 0.0, win)
    kern = h3 * jnp.concatenate([win] * HY_ORDER, axis=0)
    for o in range(HY_ORDER):
        for j in range(pc // LANES):
            kT_ref[o, j] = kern[o * LANES:(o + 1) * LANES, j * LANES:(j + 1) * LANES]
    ss = jnp.sum(kern * kern, axis=1, keepdims=True)

    @pl.when(p == 0)
    def _():
        sc_ref[0] = ss

    @pl.when(jnp.logical_and(p > 0, p < last))
    def _():
        sc_ref[0] = sc_ref[0] + ss

    @pl.when(p == last)
    def _():
        sc_ref[0] = lax.rsqrt(sc_ref[0] + ss)


def _filt(L, zT, t_all, w1T, b1, f1, w2T, b2, f2, w3T, dl):
    pc = min(FILT_PC, L)
    npc = 2 * L // pc
    ncb = HY_WIDTH // LANES
    full1 = lambda a: pl.BlockSpec(a.shape, lambda p: (0,) * a.ndim)
    h2T = pl.pallas_call(
        _filt_mlp_body,
        grid=(npc,),
        in_specs=[pl.BlockSpec((HY_HIDDEN, pc), lambda p: (0, p)),
                  full1(w1T), full1(b1), full1(f1), full1(w2T), full1(b2), full1(f2)],
        out_specs=pl.BlockSpec((HY_HIDDEN, pc), lambda p: (0, p)),
        out_shape=jax.ShapeDtypeStruct((HY_HIDDEN, 2 * L), F32),
        name="hyena_filter_mlp",
    )(zT, w1T, b1, f1, w2T, b2, f2)
    return pl.pallas_call(
        functools.partial(_filt_body, L=L, pc=pc),
        grid=(ncb, npc),
        in_specs=[
            pl.BlockSpec((HY_HIDDEN, pc), lambda c, p: (0, p)),
            pl.BlockSpec((1, pc), lambda c, p: (0, p)),
            pl.BlockSpec((1, HY_ORDER, LANES, 3 * HY_HIDDEN), lambda c, p: ((p * pc) // L, 0, c, 0)),
            pl.BlockSpec((LANES, 1), lambda c, p: (c, 0)),
        ],
        out_specs=[
            pl.BlockSpec((HY_ORDER, pc // LANES, LANES, LANES), lambda c, p: (0, p, c, 0)),
            pl.BlockSpec((1, HY_ORDER * LANES, 1), lambda c, p: (c, 0, 0)),
        ],
        out_shape=[
            jax.ShapeDtypeStruct((HY_ORDER, 2 * L // LANES, HY_WIDTH, LANES), F32),
            jax.ShapeDtypeStruct((ncb, HY_ORDER * LANES, 1), F32),
        ],
        name="hyena_filter",
    )(h2T, t_all, w3T, dl)


def _kfft_body(sc_ref, kT_ref, a1k_ref, twr_ref, twi_ref, b2_ref, kr_ref, ki_ref, x_s, l2_s, *, N1, cbn):
    o = pl.program_id(0)
    cb = pl.program_id(1)
    ks = jnp.swapaxes(kT_ref[...], 0, 1)
    for c in range(cbn):
        hi, lo = _split_bf16(ks[c] * sc_ref[o * HY_WIDTH + cb * cbn + c])
        cs = slice(c * LANES, (c + 1) * LANES)
        x_s[0:N1, cs] = hi
        x_s[N1:2 * N1, cs] = lo
        x_s[2 * N1:3 * N1, cs] = hi
    y = jnp.dot(a1k_ref[...], x_s[...], preferred_element_type=F32)
    twr = twr_ref[...]
    twi = twi_ref[...]
    for c in range(cbn):
        yr = y[0:N1, c * LANES:(c + 1) * LANES]
        yi = y[N1:2 * N1, c * LANES:(c + 1) * LANES]
        rs = slice(c * N1, (c + 1) * N1)
        for part, val in ((0, yr * twr - yi * twi), (1, yr * twi + yi * twr)):
            hi, lo = _split_bf16(val)
            l2_s[rs, part * LANES:(part + 1) * LANES] = hi
            l2_s[rs, (2 + part) * LANES:(3 + part) * LANES] = lo
            l2_s[rs, (4 + part) * LANES:(5 + part) * LANES] = hi
    z = jnp.dot(l2_s[...], b2_ref[...], preferred_element_type=F32)
    kr_ref[...] = z[:, :LANES]
    ki_ref[...] = z[:, LANES:]


def _kfft(scale, kT, tabs):
    N1 = tabs["N1"]
    cbn = _hyena_channels(N1 * LANES // 2)
    ncb = HY_WIDTH // cbn
    full = lambda a: pl.BlockSpec(a.shape, lambda o, c: (0,) * a.ndim)
    twr, twi = tabs["twr"], tabs["twi"]
    a_hi, a_lo = _split_bf16(tabs["a1k"])
    b_hi, b_lo = _split_bf16(tabs["b2"])
    a1k = jnp.concatenate([a_hi, a_hi, a_lo], axis=1)
    b2 = jnp.concatenate([b_hi, b_hi, b_lo], axis=0)
    return pl.pallas_call(
        functools.partial(_kfft_body, N1=N1, cbn=cbn),
        grid=(HY_ORDER, ncb),
        in_specs=[
            pl.BlockSpec(memory_space=pltpu.SMEM),
            pl.BlockSpec((None, N1, cbn, LANES), lambda o, c: (o, 0, c, 0)),
            full(a1k), full(twr), full(twi), full(b2),
        ],
        out_specs=[
            pl.BlockSpec((None, cbn * N1, LANES), lambda o, c: (o, c, 0)),
            pl.BlockSpec((None, cbn * N1, LANES), lambda o, c: (o, c, 0)),
        ],
        out_shape=[
            jax.ShapeDtypeStruct((HY_ORDER, HY_WIDTH * N1, LANES), F32),
            jax.ShapeDtypeStruct((HY_ORDER, HY_WIDTH * N1, LANES), F32),
        ],
        scratch_shapes=[pltpu.VMEM((3 * N1, cbn * LANES), BF16), pltpu.VMEM((cbn * N1, 6 * LANES), BF16)],
        name="hyena_kfft",
    )(scale, kT, a1k, twr, twi, b2)


def _hyena_body(hyb_ref, x1_ref, x2_ref, v_ref, kr_ref, ki_ref, a1_ref, twr_ref, twi_ref,
                b2_ref, b2c_ref, a1i_ref, o_ref, x_s, l2_s, r1_s, z_s, *, N1, H1, cbn):
    cb = pl.program_id(0)

    def slabs(ref, e):
        return jnp.swapaxes(ref[e].reshape(H1, cbn, LANES), 0, 1)

    def long_conv(order):
        twr = twr_ref[...]
        twi = twi_ref[...]
        y = jnp.dot(a1_ref[...], x_s[...], preferred_element_type=F32)
        for c in range(cbn):
            yr = y[0:N1, c * LANES:(c + 1) * LANES]
            yi = y[N1:2 * N1, c * LANES:(c + 1) * LANES]
            l2_s[c * N1:(c + 1) * N1, 0:LANES] = (yr * twr - yi * twi).astype(BF16)
            l2_s[c * N1:(c + 1) * N1, LANES:2 * LANES] = (yr * twi + yi * twr).astype(BF16)
        z = jnp.dot(l2_s[...], b2_ref[...], preferred_element_type=F32)
        zr = z[:, :LANES]
        zi = z[:, LANES:]
        kr = kr_ref[order]
        ki = ki_ref[order]
        l2_s[:, 0:LANES] = (zr * kr - zi * ki).astype(BF16)
        l2_s[:, LANES:2 * LANES] = (zr * ki + zi * kr).astype(BF16)
        v = jnp.dot(l2_s[...], b2c_ref[...], preferred_element_type=F32)
        for c in range(cbn):
            vr = v[c * N1:(c + 1) * N1, :LANES]
            vi = v[c * N1:(c + 1) * N1, LANES:]
            r1_s[0:N1, c * LANES:(c + 1) * LANES] = (vr * twr + vi * twi).astype(BF16)
            r1_s[N1:2 * N1, c * LANES:(c + 1) * LANES] = (vi * twr - vr * twi).astype(BF16)
        return jnp.dot(a1i_ref[...], r1_s[...], preferred_element_type=F32)

    for e in range(2):
        vs = slabs(v_ref, e)
        z_s[e] = vs
        for c in range(cbn):
            x_s[e * H1:(e + 1) * H1, c * LANES:(c + 1) * LANES] = vs[c].astype(BF16)
    y1 = long_conv(0)
    for e in range(2):
        x1s = slabs(x1_ref, e)
        for c in range(cbn):
            c1 = y1[e * H1:(e + 1) * H1, c * LANES:(c + 1) * LANES] + hyb_ref[cb * cbn + c] * z_s[e, c]
            z2 = x1s[c] * c1
            z_s[e, c] = z2
            x_s[e * H1:(e + 1) * H1, c * LANES:(c + 1) * LANES] = z2.astype(BF16)
    y2 = long_conv(1)
    for e in range(2):
        x2s = slabs(x2_ref, e)
        for c in range(cbn):
            c2 = (y2[e * H1:(e + 1) * H1, c * LANES:(c + 1) * LANES]
                  + hyb_ref[HY_WIDTH + cb * cbn + c] * z_s[e, c])
            o_ref[e, c] = x2s[c] * c2


def _hyena(hyT, hyb, kr, ki, tabs):
    B = hyT.shape[0]
    N1 = tabs["N1"]
    H1 = N1 // 2
    cbn = _hyena_channels(H1 * LANES)
    assert hyT.shape == (B, HY_COLS // cbn, H1 * cbn, LANES) and B % 2 == 0
    ncb = HY_WIDTH // cbn
    hy5 = hyT.reshape(B // 2, 2, HY_COLS // cbn, H1 * cbn, LANES)
    a1 = tabs["a1"].astype(BF16)
    a1i = tabs["a1i"].astype(BF16)
    b2 = tabs["b2"].astype(BF16)
    b2c = tabs["b2c"].astype(BF16)
    twr, twi = tabs["twr"], tabs["twi"]
    full = lambda a: pl.BlockSpec(a.shape, lambda c, p: (0,) * a.ndim)
    smem = pl.BlockSpec(memory_space=pltpu.SMEM)
    xspec = lambda k: pl.BlockSpec((None, 2, None, H1 * cbn, LANES), lambda c, p: (p, 0, k * ncb + c, 0, 0))
    kspec = pl.BlockSpec((HY_ORDER, cbn * N1, LANES), lambda c, p: (0, c, 0))
    out = pl.pallas_call(
        functools.partial(_hyena_body, N1=N1, H1=H1, cbn=cbn),
        grid=(ncb, B // 2),
        in_specs=[smem, xspec(0), xspec(1), xspec(2), kspec, kspec,
                  full(a1), full(twr), full(twi), full(b2), full(b2c), full(a1i)],
        out_specs=pl.BlockSpec((None, 2, cbn, H1, LANES), lambda c, p: (p, 0, c, 0, 0)),
        out_shape=jax.ShapeDtypeStruct((B // 2, 2, HY_WIDTH, H1, LANES), F32),
        scratch_shapes=[
            pltpu.VMEM((2 * H1, cbn * LANES), BF16),
            pltpu.VMEM((cbn * N1, 2 * LANES), BF16),
            pltpu.VMEM((2 * N1, cbn * LANES), BF16),
            pltpu.VMEM((2, cbn, H1, LANES), F32),
        ],
        name="hyena_mix",
    )(hyb, hy5, hy5, hy5, kr, ki, a1, twr, twi, b2, b2c, a1i)
    return out.reshape(B, HY_WIDTH, H1, LANES)


GP_ROWS = 6 * ML_HEADS


def _seg_scan(x, op, reverse, fill, T):
    W = x.shape[1]
    pos = lax.broadcasted_iota(jnp.int32, x.shape, 1) % T
    k = 1
    while k < T:
        if reverse:
            x = op(x, jnp.where(pos < T - k, pltpu.roll(x, W - k, 1), fill))
        else:
            x = op(x, jnp.where(pos >= k, pltpu.roll(x, k, 1), fill))
        k *= 2
    return x


def _log_sigmoid(x):
    return jnp.minimum(x, 0.0) - jnp.log(1.0 + jnp.exp(-jnp.abs(x)))


def _gate_prep_body(g_ref, o_ref, *, T):
    g = g_ref[0]
    for d in range(2):
        ig = g[8 * d:8 * d + ML_HEADS]
        fg = g[8 * d + ML_HEADS:8 * d + 2 * ML_HEADS]
        b = _seg_scan(_log_sigmoid(fg), jnp.add, d == 1, 0.0, T)
        a = ig - b
        o_ref[0, 4 * d:4 * d + 4, :] = a
        o_ref[0, 8 + 4 * d:12 + 4 * d, :] = b
        o_ref[0, 16 + 4 * d:20 + 4 * d, :] = _seg_scan(a, jnp.maximum, d == 1, -jnp.inf, T)


def _gate_prep(gT, T):
    B, _, S = gT.shape
    w = min(GATE_W, S)
    return pl.pallas_call(
        functools.partial(_gate_prep_body, T=T),
        grid=(B, S // w),
        in_specs=[pl.BlockSpec((1, N_GATE, w), lambda b, i: (b, 0, i))],
        out_specs=pl.BlockSpec((1, GP_ROWS, w), lambda b, i: (b, 0, i)),
        out_shape=jax.ShapeDtypeStruct((B, GP_ROWS, S), F32),
        name="mlstm_gates",
    )(gT)


def _gp_dir(gp, d, T):
    a = gp[4 * d:4 * d + 4]
    b = gp[8 + 4 * d:12 + 4 * d]
    cm = gp[16 + 4 * d:20 + 4 * d]
    e = 0 if d == 1 else T - 1
    return a, b, cm, b[:, e:e + 1], cm[:, e:e + 1]


def _mlstm_state_body(gf_ref, gb_ref, kf_ref, vf_ref, kb_ref, vb_ref, sf_ref, mf_ref, sb_ref, mb_ref,
                      s_s, m_s, *, T):
    i = pl.program_id(1)

    @pl.when(i == 0)
    def _():
        s_s[...] = jnp.zeros(s_s.shape, F32)
        m_s[...] = jnp.zeros(m_s.shape, F32)

    for d, (g_ref, k_ref, v_ref, s_out, m_out) in enumerate(
            ((gf_ref, kf_ref, vf_ref, sf_ref, mf_ref), (gb_ref, kb_ref, vb_ref, sb_ref, mb_ref))):
        a, _, _, btot, amax = _gp_dir(g_ref[0], d, T)
        w = jnp.exp(a - amax)
        m_old = m_s[d, 0:ML_HEADS, 0:1]
        m_out[0, 0] = m_s[d]
        m_new = jnp.maximum(btot + m_old, btot + amax)
        s_old = jnp.exp(btot + m_old - m_new)
        s_loc = jnp.exp(btot + amax - m_new)
        m_s[d, 0:ML_HEADS, :] = jnp.broadcast_to(m_new, (ML_HEADS, LANES))
        for h in range(ML_HEADS):
            hs = slice(h * ML_HEAD_DIM, (h + 1) * ML_HEAD_DIM)
            wr = w[h:h + 1, :]
            vw = jnp.concatenate([(v_ref[0, hs, :].astype(F32) * wr).astype(BF16),
                                  jnp.broadcast_to(wr, (ML_HEAD_DIM, T)).astype(BF16)], axis=0)
            c_loc = jnp.dot(vw, k_ref[0, :, hs], preferred_element_type=F32)
            st = s_s[d, h]
            s_out[0, 0, h] = st.astype(BF16)
            s_s[d, h] = s_old[h:h + 1, :] * st + s_loc[h:h + 1, :] * c_loc


def _mlstm_state(gp, k, vT, T):
    B, S, _ = k.shape
    nc = S // T
    sd = 2 * ML_HEAD_DIM
    pick = lambda rev, f: (lambda b, i: f(b, nc - 1 - i)) if rev else (lambda b, i: f(b, i))
    gspec = lambda rev: pl.BlockSpec((1, GP_ROWS, T), pick(rev, lambda b, c: (b, 0, c)))
    kspec = lambda rev: pl.BlockSpec((1, T, ML_WIDTH), pick(rev, lambda b, c: (b, c, 0)))
    vspec = lambda rev: pl.BlockSpec((1, ML_WIDTH, T), pick(rev, lambda b, c: (b, 0, c)))
    sspec = lambda rev: pl.BlockSpec((1, 1, ML_HEADS, sd, ML_HEAD_DIM), pick(rev, lambda b, c: (b, c, 0, 0, 0)))
    mspec = lambda rev: pl.BlockSpec((1, 1, SUBLANES, LANES), pick(rev, lambda b, c: (b, c, 0, 0)))
    return pl.pallas_call(
        functools.partial(_mlstm_state_body, T=T),
        grid=(B, nc),
        in_specs=[gspec(False), gspec(True), kspec(False), vspec(False), kspec(True), vspec(True)],
        out_specs=[sspec(False), mspec(False), sspec(True), mspec(True)],
        out_shape=[
            jax.ShapeDtypeStruct((B, nc, ML_HEADS, sd, ML_HEAD_DIM), BF16),
            jax.ShapeDtypeStruct((B, nc, SUBLANES, LANES), F32),
            jax.ShapeDtypeStruct((B, nc, ML_HEADS, sd, ML_HEAD_DIM), BF16),
            jax.ShapeDtypeStruct((B, nc, SUBLANES, LANES), F32),
        ],
        scratch_shapes=[pltpu.VMEM((2, ML_HEADS, sd, ML_HEAD_DIM), F32), pltpu.VMEM((2, SUBLANES, LANES), F32)],
        name="mlstm_state",
    )(gp, gp, k, vT, k, vT)


def _mlstm_out_body(g_ref, q_ref, k_ref, v_ref, o_ref, sf_ref, mf_ref, sb_ref, mb_ref, ng_ref, y_ref, *, T):
    for u in range(g_ref.shape[2] // T):
        _mlstm_out_chunk(u, g_ref, q_ref, k_ref, v_ref, o_ref, sf_ref, mf_ref, sb_ref, mb_ref, ng_ref, y_ref, T)


def _mlstm_out_chunk(u, g_ref, q_ref, k_ref, v_ref, o_ref, sf_ref, mf_ref, sb_ref, mb_ref, ng_ref, y_ref, T):
    ts = slice(u * T, (u + 1) * T)
    gp = g_ref[0, :, ts]
    rows = []
    for d, m_ref in enumerate((mf_ref, mb_ref)):
        a, b, cm, _, _ = _gp_dir(gp, d, T)
        m = m_ref[0, u][0:ML_HEADS, 0:1]
        mt = jnp.maximum(cm, m)
        rows.append((a, mt, jnp.exp(-(b + mt)), jnp.exp(m - mt)))
    a_cols = jnp.concatenate([rows[0][0], rows[1][0],
                              jnp.zeros((LANES - 2 * ML_HEADS, T), F32)], axis=0).T
    ss = lax.broadcasted_iota(jnp.int32, (T, T), 0)
    tt = lax.broadcasted_iota(jnp.int32, (T, T), 1)
    masks = (ss <= tt, ss >= tt)
    ones = jnp.ones((ML_HEAD_DIM, T), BF16)
    hsl = [slice(h * ML_HEAD_DIM, (h + 1) * ML_HEAD_DIM) for h in range(ML_HEADS)]
    ps, qss, v1s = [], [], []
    for h in range(ML_HEADS):
        qT = q_ref[0, hsl[h], ts]
        ps.append(jnp.dot(k_ref[0, ts, hsl[h]], qT, preferred_element_type=F32))
        s_fb = jnp.concatenate([sf_ref[0, u, h], sb_ref[0, u, h]], axis=0)
        qss.append(jnp.dot(s_fb, qT, preferred_element_type=F32))
        v1s.append(jnp.concatenate([v_ref[0, hsl[h], ts], ones], axis=0))
    sms = []
    for h in range(ML_HEADS):
        for d in range(2):
            a_c = a_cols[:, ML_HEADS * d + h:ML_HEADS * d + h + 1]
            e = jnp.exp(jnp.where(masks[d], a_c - rows[d][1][h:h + 1, :], -jnp.inf))
            sms.append((ps[h] * e).astype(BF16))
    accs = [jnp.dot(v1s[h], sms[2 * h + d], preferred_element_type=F32)
            for h in range(ML_HEADS) for d in range(2)]
    sd = 2 * ML_HEAD_DIM
    for h in range(ML_HEADS):
        hm = None
        for d in range(2):
            emt_r = rows[d][2][h:h + 1, :]
            int_r = rows[d][3][h:h + 1, :]
            acc = accs[2 * h + d] + int_r * qss[h][sd * d:sd * (d + 1)]
            den = acc[ML_HEAD_DIM:ML_HEAD_DIM + SUBLANES]
            rden = 1.0 / jnp.maximum(jnp.abs(den), emt_r)
            hd = acc[:ML_HEAD_DIM] * jnp.concatenate([rden] * (ML_HEAD_DIM // SUBLANES), axis=0)
            hm = hd if hm is None else hm + hd
        hn = hm * lax.rsqrt(jnp.mean(hm * hm, axis=0, keepdims=True) + EPS)
        y_ref[0, hsl[h], ts] = hn * ng_ref[hsl[h], :] * o_ref[0, hsl[h], ts]


def _mlstm_out(gp, qT, k, vT, oT, sf, mf, sb, mb, ng, T):
    B, S, _ = k.shape
    nc = S // T
    sd = 2 * ML_HEAD_DIM
    cps = ML_OUT_CHUNKS if nc % ML_OUT_CHUNKS == 0 else 1
    tw = cps * T
    tspec = pl.BlockSpec((1, ML_WIDTH, tw), lambda b, i: (b, 0, i))
    sspec = pl.BlockSpec((1, cps, ML_HEADS, sd, ML_HEAD_DIM), lambda b, i: (b, i, 0, 0, 0))
    mspec = pl.BlockSpec((1, cps, SUBLANES, LANES), lambda b, i: (b, i, 0, 0))
    return pl.pallas_call(
        functools.partial(_mlstm_out_body, T=T),
        grid=(B, nc // cps),
        in_specs=[pl.BlockSpec((1, GP_ROWS, tw), lambda b, i: (b, 0, i)), tspec,
                  pl.BlockSpec((1, tw, ML_WIDTH), lambda b, i: (b, i, 0)), tspec, tspec,
                  sspec, mspec, sspec, mspec, pl.BlockSpec((ML_WIDTH, 1), lambda b, i: (0, 0))],
        out_specs=tspec,
        out_shape=jax.ShapeDtypeStruct((B, ML_WIDTH, S), F32),
        name="mlstm_out",
    )(gp, qT, k, vT, oT, sf, mf, sb, mb, ng)


def _out_proj_body(x_ref, yhy_ref, yml_ref, why_ref, wml_ref, o_ref, *, tm):
    yt = jnp.concatenate([yhy_ref[0, :, j, :].T for j in range(tm // LANES)], axis=0)
    acc = jnp.dot(yt.astype(BF16), why_ref[...], preferred_element_type=F32)
    acc = acc + jnp.dot(yml_ref[0].T.astype(BF16), wml_ref[...], preferred_element_type=F32)
    o_ref[0] = x_ref[0] + acc


def _out_proj(x, yhy, yml, why, wml):
    B, S, D = x.shape
    tm = min(OUT_TM, S)
    full = lambda a: pl.BlockSpec(a.shape, lambda b, i: (0,) * a.ndim)
    return pl.pallas_call(
        functools.partial(_out_proj_body, tm=tm),
        grid=(B, S // tm),
        in_specs=[
            pl.BlockSpec((1, tm, D), lambda b, i: (b, i, 0)),
            pl.BlockSpec((1, HY_WIDTH, tm // LANES, LANES), lambda b, i: (b, 0, i, 0)),
            pl.BlockSpec((1, ML_WIDTH, tm), lambda b, i: (b, 0, i)),
            full(why), full(wml),
        ],
        out_specs=pl.BlockSpec((1, tm, D), lambda b, i: (b, i, 0)),
        out_shape=jax.ShapeDtypeStruct((B, S, D), F32),
        name="out_proj",
    )(x, yhy, yml, why, wml)


def _ffn_body(xp_ref, x_ref, xn_ref, g2_ref, wup_ref, cw_ref, cb_ref, wdn_ref, gf_ref, o_ref, act_s,
              *, tm, ft, final_norm):
    i = pl.program_id(1)
    nt = pl.num_programs(1)
    g2 = g2_ref[...]
    xm = x_ref[0]
    h = _rms(xm, g2)
    hp = jnp.where(i > 0, _rms(xp_ref[0], g2), 0.0)
    hn = jnp.where(i < nt - 1, _rms(xn_ref[0], g2), 0.0)
    hext = jnp.concatenate([hp, h, hn], axis=0).astype(BF16)
    for j in range(D_FF // ft):
        cv = slice(j * ft, (j + 1) * ft)
        cg = slice(D_FF + j * ft, D_FF + (j + 1) * ft)
        uv = jnp.dot(hext, wup_ref[:, cv], preferred_element_type=F32)
        ug = jnp.dot(hext, wup_ref[:, cg], preferred_element_type=F32)
        val = _conv3_rows(uv, cw_ref[:, cv], cb_ref[:, cv], tm)
        gate = _conv3_rows(ug, cw_ref[:, cg], cb_ref[:, cg], tm)
        act_s[:, cv] = (gate * _sigmoid(gate) * val).astype(BF16)
    xo = xm + jnp.dot(act_s[...], wdn_ref[...], preferred_element_type=F32)
    o_ref[0] = _rms(xo, gf_ref[...]) if final_norm else xo


def _ffn(x, g2, wup, cw, cb, wdn, gf, final_norm):
    B, S, D = x.shape
    tm = min(FFN_TM, S)
    nt = S // tm
    r8 = tm // SUBLANES
    full = lambda a: pl.BlockSpec(a.shape, lambda b, i: (0,) * a.ndim)
    once = lambda a: pl.BlockSpec(a.shape, lambda b, i: (0,) * a.ndim, pipeline_mode=pl.Buffered(1))
    return pl.pallas_call(
        functools.partial(_ffn_body, tm=tm, ft=FFN_FT, final_norm=final_norm),
        grid=(B, nt),
        in_specs=[
            pl.BlockSpec((1, SUBLANES, D), lambda b, i: (b, jnp.maximum(i * r8 - 1, 0), 0)),
            pl.BlockSpec((1, tm, D), lambda b, i: (b, i, 0)),
            pl.BlockSpec((1, SUBLANES, D), lambda b, i: (b, jnp.minimum((i + 1) * r8, S // SUBLANES - 1), 0)),
            full(g2), once(wup), full(cw), full(cb), once(wdn), full(gf),
        ],
        out_specs=pl.BlockSpec((1, tm, D), lambda b, i: (b, i, 0)),
        out_shape=jax.ShapeDtypeStruct((B, S, D), F32),
        scratch_shapes=[pltpu.VMEM((tm, D_FF), BF16)],
        name="ffn",
    )(x, x, x, g2, wup, cw, cb, wdn, gf)


def _layer_params(l, norm1_g, w_in, conv_w, conv_b, gate_b, filt_w1, filt_b1, filt_freq1, filt_w2, filt_b2,
                  filt_freq2, filt_w3, hy_bias, ml_norm_g, w_out, norm2_g, w_up, ffn_conv_w, ffn_conv_b, w_down):
    w = w_in[l]
    col = lambda a: a.astype(F32).reshape(-1, 1)
    return dict(
        g1=norm1_g[l].reshape(1, D_MODEL),
        wc=w[:, :CONV_CH].astype(BF16),
        wvo=w[:, OFF_V:].T.astype(BF16),
        cw=conv_w[l], cb=conv_b[l].reshape(1, -1),
        gb=col(gate_b[l]),
        hyb=hy_bias[l].reshape(-1),
        w1T=_split_cols(jnp.pad(filt_w1[l], ((0, HY_HIDDEN - HY_EMB), (0, 0))).T),
        b1=col(filt_b1[l]), f1=col(filt_freq1[l]),
        w2T=_split_cols(filt_w2[l].T), b2=col(filt_b2[l]), f2=col(filt_freq2[l]),
        w3T=_split_cols(filt_w3[l].T).reshape(2, HY_ORDER, HY_WIDTH, 3 * HY_HIDDEN),
        ng=col(ml_norm_g[l]),
        why_o=w_out[l][:HY_WIDTH].astype(BF16), wml_o=w_out[l][HY_WIDTH:].astype(BF16),
        g2=norm2_g[l].reshape(1, D_MODEL),
        wup=w_up[l].astype(BF16), cw_f=ffn_conv_w[l], cb_f=ffn_conv_b[l].reshape(1, -1),
        wdn=w_down[l].astype(BF16),
    )


def _hyena_spectrum(L, p, tabs):
    zT, t_all = _filter_features(L)
    deltas = jnp.abs(jnp.linspace(HY_MIN_DECAY, HY_MAX_DECAY, HY_WIDTH, dtype=F32)).reshape(-1, 1)
    kT, sc = _filt(L, zT, t_all, p["w1T"], p["b1"], p["f1"], p["w2T"], p["b2"], p["f2"], p["w3T"], deltas)
    ncb = HY_WIDTH // LANES
    scale = sc.reshape(ncb, HY_ORDER, LANES).transpose(1, 0, 2).reshape(-1)
    return _kfft(scale, kT, tabs)


def _encoder_layer(x, p, tabs, final_g):
    B, S, _ = x.shape
    hyT, gT, q, k, v, o = _in_proj(x, p["g1"], p["wc"], p["wvo"], p["cw"], p["cb"], p["gb"])
    kr, ki = _hyena_spectrum(S, p, tabs)
    y_hy = _hyena(hyT, p["hyb"], kr, ki, tabs)
    T = min(ML_CHUNK, S)
    gp = _gate_prep(gT, T)
    sf, mf, sb, mb = _mlstm_state(gp, k, v, T)
    y_ml = _mlstm_out(gp, q, k, v, o, sf, mf, sb, mb, p["ng"], T)
    xm = _out_proj(x, y_hy, y_ml, p["why_o"], p["wml_o"])
    gf = p["g2"] if final_g is None else final_g.reshape(1, D_MODEL)
    return _ffn(xm, p["g2"], p["wup"], p["cw_f"], p["cb_f"], p["wdn"], gf, final_g is not None)


def _trunk(x, layers, final_g):
    tabs = _dft_tables(x.shape[1])
    for l, p in enumerate(layers):
        x = _encoder_layer(x, p, tabs, final_g if l == len(layers) - 1 else None)
    return x


def kernel(x_prompt, x_sample, norm1_g, w_in, conv_w, conv_b, gate_b, filt_w1, filt_b1, filt_freq1, filt_w2,
           filt_b2, filt_freq2, filt_w3, hy_bias, ml_norm_g, w_out, norm2_g, w_up, ffn_conv_w, ffn_conv_b,
           w_down, final_g):
    weights = (norm1_g, w_in, conv_w, conv_b, gate_b, filt_w1, filt_b1, filt_freq1, filt_w2, filt_b2,
               filt_freq2, filt_w3, hy_bias, ml_norm_g, w_out, norm2_g, w_up, ffn_conv_w, ffn_conv_b, w_down)
    layers = [_layer_params(l, *weights) for l in range(norm1_g.shape[0])]
    return (_trunk(x_prompt, layers, final_g), _trunk(x_sample, layers, final_g))
```

```python
import functools
import math

import jax
import jax.numpy as jnp
import numpy as np
from jax import lax
from jax.experimental import pallas as pl
from jax.experimental.pallas import tpu as pltpu

F32 = jnp.float32
BF16 = jnp.bfloat16

LANES = 128
SUBLANES = 8

D_MODEL = 1024
HY_WIDTH = 512
ML_WIDTH = 512
ML_HEADS = 4
ML_HEAD_DIM = ML_WIDTH // ML_HEADS
HY_ORDER = 2
HY_EMB = 33
HY_BANDS = (HY_EMB - 1) // 2
HY_HIDDEN = 64
HY_DECAY_TARGET = 1e-2
HY_MAX_DECAY = math.log(HY_DECAY_TARGET) / 0.3
HY_MIN_DECAY = math.log(HY_DECAY_TARGET) / 1.5
D_FF = 2816
EPS = 1e-6

HY_COLS = 3 * HY_WIDTH
OFF_Q = HY_COLS
OFF_K = OFF_Q + ML_WIDTH
OFF_V = OFF_K + ML_WIDTH
OFF_O = OFF_V + ML_WIDTH
OFF_G = OFF_O + ML_WIDTH
N_GATE = 4 * ML_HEADS
CONV_CH = OFF_V

ML_CHUNK = 256
ML_OUT_CHUNKS = 4
ML_STATE_CHUNKS = 8
GATE_W = 4096
IN_TM = 512
CONV_GW = 256
OUT_TM = 1024
FFN_TM = 512
FFN_FT = 256
FILT_PC = 2048
FFT_ROWS = 2048


def _rms(xv, g):
    ms = jnp.mean(xv * xv, axis=-1, keepdims=True)
    return xv * lax.rsqrt(ms + EPS) * g


def _sigmoid(x):
    return 1.0 / (1.0 + jnp.exp(-x))


def _conv3_rows(p, w, b, tm):
    n = p.shape[0]
    dn = pltpu.roll(p, 1, 0)[SUBLANES:tm + SUBLANES]
    up = pltpu.roll(p, n - 1, 0)[SUBLANES:tm + SUBLANES]
    mid = p[SUBLANES:tm + SUBLANES]
    return dn * w[0:1] + mid * w[1:2] + up * w[2:3] + b


def _in_proj_body(xp_ref, x_ref, xn_ref, g1_ref, wc_ref, wvo_ref, cw_ref, cb_ref, gb_ref,
                  hyT_ref, gT_ref, q_ref, k_ref, v_ref, o_ref, *, tm, cbn):
    i = pl.program_id(1)
    nt = pl.num_programs(1)
    g1 = g1_ref[...]
    h = _rms(x_ref[0], g1)
    hp = jnp.where(i > 0, _rms(xp_ref[0], g1), 0.0)
    hn = jnp.where(i < nt - 1, _rms(xn_ref[0], g1), 0.0)
    hext = jnp.concatenate([hp, h, hn], axis=0).astype(BF16)
    for g in range(CONV_CH // CONV_GW):
        cs = slice(g * CONV_GW, (g + 1) * CONV_GW)
        pc = jnp.dot(hext, wc_ref[:, cs], preferred_element_type=F32)
        c = _conv3_rows(pc, cw_ref[:, cs], cb_ref[:, cs], tm)
        if g < HY_COLS // CONV_GW:
            cT = c.T
            for j in range(tm // LANES):
                for blk in range(CONV_GW // cbn):
                    hyT_ref[0, g * (CONV_GW // cbn) + blk, j * cbn:(j + 1) * cbn, :] = (
                        cT[blk * cbn:(blk + 1) * cbn, j * LANES:(j + 1) * LANES])
        elif g < OFF_K // CONV_GW:
            q_ref[0, g * CONV_GW - OFF_Q:(g + 1) * CONV_GW - OFF_Q, :] = (c * _sigmoid(c)).T.astype(BF16)
        else:
            k_ref[0, :, g * CONV_GW - OFF_K:(g + 1) * CONV_GW - OFF_K] = (
                c * _sigmoid(c) * (ML_HEAD_DIM ** -0.5)).astype(BF16)
    rT = lax.dot_general(wvo_ref[...], h.astype(BF16), (((1,), (1,)), ((), ())),
                         preferred_element_type=F32)
    v_ref[0] = rT[:ML_WIDTH].astype(BF16)
    o_ref[0] = _sigmoid(rT[ML_WIDTH:2 * ML_WIDTH])
    gT_ref[0] = rT[2 * ML_WIDTH:] + gb_ref[...]


def _hyena_channels(S):
    return min(FFT_ROWS // (2 * S // LANES), 32)


def _in_proj(x, g1, wc, wvo, cw, cb, gb):
    B, S, D = x.shape
    cbn = _hyena_channels(S)
    tm = min(IN_TM, S)
    nt = S // tm
    r8 = tm // SUBLANES
    full = lambda shape: pl.BlockSpec(shape, lambda b, i: (0,) * len(shape))
    return pl.pallas_call(
        functools.partial(_in_proj_body, tm=tm, cbn=cbn),
        grid=(B, nt),
        in_specs=[
            pl.BlockSpec((1, SUBLANES, D), lambda b, i: (b, jnp.maximum(i * r8 - 1, 0), 0)),
            pl.BlockSpec((1, tm, D), lambda b, i: (b, i, 0)),
            pl.BlockSpec((1, SUBLANES, D), lambda b, i: (b, jnp.minimum((i + 1) * r8, S // SUBLANES - 1), 0)),
            full((1, D)), full(wc.shape), full(wvo.shape), full(cw.shape), full(cb.shape), full(gb.shape),
        ],
        out_specs=[
            pl.BlockSpec((1, HY_COLS // cbn, (tm // LANES) * cbn, LANES), lambda b, i: (b, 0, i, 0)),
            pl.BlockSpec((1, N_GATE, tm), lambda b, i: (b, 0, i)),
            pl.BlockSpec((1, ML_WIDTH, tm), lambda b, i: (b, 0, i)),
            pl.BlockSpec((1, tm, ML_WIDTH), lambda b, i: (b, i, 0)),
            pl.BlockSpec((1, ML_WIDTH, tm), lambda b, i: (b, 0, i)),
            pl.BlockSpec((1, ML_WIDTH, tm), lambda b, i: (b, 0, i)),
        ],
        out_shape=[
            jax.ShapeDtypeStruct((B, HY_COLS // cbn, (S // LANES) * cbn, LANES), F32),
            jax.ShapeDtypeStruct((B, N_GATE, S), F32),
            jax.ShapeDtypeStruct((B, ML_WIDTH, S), BF16),
            jax.ShapeDtypeStruct((B, S, ML_WIDTH), BF16),
            jax.ShapeDtypeStruct((B, ML_WIDTH, S), BF16),
            jax.ShapeDtypeStruct((B, ML_WIDTH, S), F32),
        ],
        name="in_proj",
    )(x, x, x, g1, wc, wvo, cw, cb, gb)


def _dft_tables(L):
    N = 2 * L
    N1 = N // LANES
    H1 = N1 // 2

    def cs(num, den):
        ang = (2.0 * np.pi / den) * (num % den)
        return np.cos(ang), np.sin(ang)

    k1 = np.arange(N1, dtype=np.int64)[:, None]
    c, s = cs(k1 * np.arange(H1, dtype=np.int64)[None, :], N1)
    a1 = np.block([[c, s], [-s, c]])
    a1i = np.block([[c.T, -s.T], [s.T, c.T]]) * (1.0 / N)
    cf, sf = cs(k1 * np.arange(N1, dtype=np.int64)[None, :], N1)
    a1k = np.concatenate([cf, -sf], axis=0)
    ct, st = cs(k1 * np.arange(LANES, dtype=np.int64)[None, :], N)
    n2 = np.arange(LANES, dtype=np.int64)
    cg, sg = cs(n2[:, None] * n2[None, :], LANES)
    b2 = np.block([[cg, -sg], [sg, cg]])
    b2c = np.block([[cg, sg], [-sg, cg]])
    tabs = dict(a1=a1, a1i=a1i, a1k=a1k, twr=ct, twi=-st, b2=b2, b2c=b2c)
    tabs = {k: jnp.asarray(v.astype(np.float32)) for k, v in tabs.items()}
    return dict(N1=N1, H1=H1, **tabs)


def _filter_features(L):
    t = np.linspace(0.0, 1.0, L)[:, None]
    freqs = np.linspace(1e-4, HY_BANDS - 1, HY_BANDS)
    ang = (2.0 * np.pi / L) * np.arange(L, dtype=np.float64)[:, None] * freqs[None, :]
    z = np.concatenate([t, np.cos(ang), -np.sin(ang)], axis=-1)
    z_all = np.concatenate([z, z[0:1], z[L - 1:0:-1]], axis=0)
    t_all = np.concatenate([t, t[0:1], t[L - 1:0:-1]], axis=0)
    zT = np.pad(z_all, ((0, 0), (0, HY_HIDDEN - HY_EMB))).T
    return jnp.asarray(zT.astype(np.float32)), jnp.asarray(t_all.T.astype(np.float32))


def _split_bf16(x):
    hi = x.astype(BF16)
    return hi, (x - hi.astype(F32)).astype(BF16)


def _dot_split(w3, x):
    hi, lo = _split_bf16(x)
    return jnp.dot(w3, jnp.concatenate([hi, lo, hi], axis=0), preferred_element_type=F32)


def _split_cols(w):
    hi, lo = _split_bf16(w)
    return jnp.concatenate([hi, hi, lo], axis=-1)


def _filt_mlp_body(zT_ref, w1T_ref, b1_ref, f1_ref, w2T_ref, b2_ref, f2_ref, h2_ref):
    h1 = jnp.sin(f1_ref[...] * (_dot_split(w1T_ref[...], zT_ref[...]) + b1_ref[...]))
    h2_ref[...] = jnp.sin(f2_ref[...] * (_dot_split(w2T_ref[...], h1) + b2_ref[...]))


def _filt_body(h2_ref, t_ref, w3T_ref, dl_ref, kT_ref, sc_ref, *, L, pc):
    p = pl.program_id(1)
    last = pl.num_programs(1) - 1
    w3 = w3T_ref[0].reshape(HY_ORDER * LANES, 3 * HY_HIDDEN)
    h3 = _dot_split(w3, h2_ref[...])
    win = jnp.exp(-t_ref[...] * dl_ref[...])
    n_idx = p * pc + lax.broadcasted_iota(jnp.int32, (1, pc), 1)
    win = jnp.where(n_idx == L, 0.0, win)
    kern = h3 * jnp.concatenate([win] * HY_ORDER, axis=0)
    for o in range(HY_ORDER):
        for j in range(pc // LANES):
            kT_ref[o, j] = kern[o * LANES:(o + 1) * LANES, j * LANES:(j + 1) * LANES]
    ss = jnp.sum(kern * kern, axis=1, keepdims=True)

    @pl.when(p == 0)
    def _():
        sc_ref[0] = ss

    @pl.when(jnp.logical_and(p > 0, p < last))
    def _():
        sc_ref[0] = sc_ref[0] + ss

    @pl.when(p == last)
    def _():
        sc_ref[0] = lax.rsqrt(sc_ref[0] + ss)


def _filt(L, zT, t_all, w1T, b1, f1, w2T, b2, f2, w3T, dl):
    pc = min(FILT_PC, L)
    npc = 2 * L // pc
    ncb = HY_WIDTH // LANES
    full1 = lambda a: pl.BlockSpec(a.shape, lambda p: (0,) * a.ndim)
    h2T = pl.pallas_call(
        _filt_mlp_body,
        grid=(npc,),
        in_specs=[pl.BlockSpec((HY_HIDDEN, pc), lambda p: (0, p)),
                  full1(w1T), full1(b1), full1(f1), full1(w2T), full1(b2), full1(f2)],
        out_specs=pl.BlockSpec((HY_HIDDEN, pc), lambda p: (0, p)),
        out_shape=jax.ShapeDtypeStruct((HY_HIDDEN, 2 * L), F32),
        name="hyena_filter_mlp",
    )(zT, w1T, b1, f1, w2T, b2, f2)
    return pl.pallas_call(
        functools.partial(_filt_body, L=L, pc=pc),
        grid=(ncb, npc),
        in_specs=[
            pl.BlockSpec((HY_HIDDEN, pc), lambda c, p: (0, p)),
            pl.BlockSpec((1, pc), lambda c, p: (0, p)),
            pl.BlockSpec((1, HY_ORDER, LANES, 3 * HY_HIDDEN), lambda c, p: ((p * pc) // L, 0, c, 0)),
            pl.BlockSpec((LANES, 1), lambda c, p: (c, 0)),
        ],
        out_specs=[
            pl.BlockSpec((HY_ORDER, pc // LANES, LANES, LANES), lambda c, p: (0, p, c, 0)),
            pl.BlockSpec((1, HY_ORDER * LANES, 1), lambda c, p: (c, 0, 0)),
        ],
        out_shape=[
            jax.ShapeDtypeStruct((HY_ORDER, 2 * L // LANES, HY_WIDTH, LANES), F32),
            jax.ShapeDtypeStruct((ncb, HY_ORDER * LANES, 1), F32),
        ],
        name="hyena_filter",
    )(h2T, t_all, w3T, dl)


def _kfft_body(sc_ref, kT_ref, a1k_ref, twr_ref, twi_ref, b2_ref, kr_ref, ki_ref, x_s, l2_s, *, N1, cbn):
    o = pl.program_id(0)
    cb = pl.program_id(1)
    ks = jnp.swapaxes(kT_ref[...], 0, 1)
    for c in range(cbn):
        hi, lo = _split_bf16(ks[c] * sc_ref[o * HY_WIDTH + cb * cbn + c])
        cs = slice(c * LANES, (c + 1) * LANES)
        x_s[0:N1, cs] = hi
        x_s[N1:2 * N1, cs] = lo
        x_s[2 * N1:3 * N1, cs] = hi
    y = jnp.dot(a1k_ref[...], x_s[...], preferred_element_type=F32)
    twr = twr_ref[...]
    twi = twi_ref[...]
    for c in range(cbn):
        yr = y[0:N1, c * LANES:(c + 1) * LANES]
        yi = y[N1:2 * N1, c * LANES:(c + 1) * LANES]
        rs = slice(c * N1, (c + 1) * N1)
        for part, val in ((0, yr * twr - yi * twi), (1, yr * twi + yi * twr)):
            hi, lo = _split_bf16(val)
            l2_s[rs, part * LANES:(part + 1) * LANES] = hi
            l2_s[rs, (2 + part) * LANES:(3 + part) * LANES] = lo
            l2_s[rs, (4 + part) * LANES:(5 + part) * LANES] = hi
    z = jnp.dot(l2_s[...], b2_ref[...], preferred_element_type=F32)
    kr_ref[...] = z[:, :LANES].astype(BF16)
    ki_ref[...] = z[:, LANES:].astype(BF16)


def _kfft(scale, kT, tabs):
    N1 = tabs["N1"]
    cbn = _hyena_channels(N1 * LANES // 2)
    ncb = HY_WIDTH // cbn
    full = lambda a: pl.BlockSpec(a.shape, lambda o, c: (0,) * a.ndim)
    twr, twi = tabs["twr"], tabs["twi"]
    a_hi, a_lo = _split_bf16(tabs["a1k"])
    b_hi, b_lo = _split_bf16(tabs["b2"])
    a1k = jnp.concatenate([a_hi, a_hi, a_lo], axis=1)
    b2 = jnp.concatenate([b_hi, b_hi, b_lo], axis=0)
    return pl.pallas_call(
        functools.partial(_kfft_body, N1=N1, cbn=cbn),
        grid=(HY_ORDER, ncb),
        in_specs=[
            pl.BlockSpec(memory_space=pltpu.SMEM),
            pl.BlockSpec((None, N1, cbn, LANES), lambda o, c: (o, 0, c, 0)),
            full(a1k), full(twr), full(twi), full(b2),
        ],
        out_specs=[
            pl.BlockSpec((None, cbn * N1, LANES), lambda o, c: (o, c, 0)),
            pl.BlockSpec((None, cbn * N1, LANES), lambda o, c: (o, c, 0)),
        ],
        out_shape=[
            jax.ShapeDtypeStruct((HY_ORDER, HY_WIDTH * N1, LANES), BF16),
            jax.ShapeDtypeStruct((HY_ORDER, HY_WIDTH * N1, LANES), BF16),
        ],
        scratch_shapes=[pltpu.VMEM((3 * N1, cbn * LANES), BF16), pltpu.VMEM((cbn * N1, 6 * LANES), BF16)],
        name="hyena_kfft",
    )(scale, kT, a1k, twr, twi, b2)


def _hyena_body(hyb_ref, x1_ref, x2_ref, v_ref, kr_ref, ki_ref, a1_ref, twr_ref, twi_ref,
                b2_ref, b2c_ref, a1i_ref, o_ref, x_s, l2_s, r1_s, z_s, *, N1, H1, cbn):
    cb = pl.program_id(0)

    def slabs(ref, e):
        return jnp.swapaxes(ref[e].reshape(H1, cbn, LANES), 0, 1)

    def long_conv(order):
        twr = twr_ref[...]
        twi = twi_ref[...]
        y = jnp.dot(a1_ref[...], x_s[...], preferred_element_type=F32).astype(BF16)
        for c in range(cbn):
            yr = y[0:N1, c * LANES:(c + 1) * LANES]
            yi = y[N1:2 * N1, c * LANES:(c + 1) * LANES]
            l2_s[c * N1:(c + 1) * N1, 0:LANES] = yr * twr - yi * twi
            l2_s[c * N1:(c + 1) * N1, LANES:2 * LANES] = yr * twi + yi * twr
        z = jnp.dot(l2_s[...], b2_ref[...], preferred_element_type=F32).astype(BF16)
        zr = z[:, :LANES]
        zi = z[:, LANES:]
        kr = kr_ref[order]
        ki = ki_ref[order]
        l2_s[:, 0:LANES] = zr * kr - zi * ki
        l2_s[:, LANES:2 * LANES] = zr * ki + zi * kr
        v = jnp.dot(l2_s[...], b2c_ref[...], preferred_element_type=F32).astype(BF16)
        for c in range(cbn):
            vr = v[c * N1:(c + 1) * N1, :LANES]
            vi = v[c * N1:(c + 1) * N1, LANES:]
            r1_s[0:N1, c * LANES:(c + 1) * LANES] = vr * twr + vi * twi
            r1_s[N1:2 * N1, c * LANES:(c + 1) * LANES] = vi * twr - vr * twi
        return jnp.dot(a1i_ref[...], r1_s[...], preferred_element_type=F32)

    for e in range(2):
        vs = slabs(v_ref, e)
        z_s[e] = vs
        for c in range(cbn):
            x_s[e * H1:(e + 1) * H1, c * LANES:(c + 1) * LANES] = vs[c].astype(BF16)
    y1 = long_conv(0)
    for e in range(2):
        x1s = slabs(x1_ref, e)
        for c in range(cbn):
            c1 = y1[e * H1:(e + 1) * H1, c * LANES:(c + 1) * LANES] + hyb_ref[cb * cbn + c] * z_s[e, c]
            z2 = x1s[c] * c1
            z_s[e, c] = z2
            x_s[e * H1:(e + 1) * H1, c * LANES:(c + 1) * LANES] = z2.astype(BF16)
    y2 = long_conv(1)
    for e in range(2):
        x2s = slabs(x2_ref, e)
        for c in range(cbn):
            c2 = (y2[e * H1:(e + 1) * H1, c * LANES:(c + 1) * LANES]
                  + hyb_ref[HY_WIDTH + cb * cbn + c] * z_s[e, c])
            o_ref[e, c] = x2s[c] * c2


def _hyena(hyT, hyb, kr, ki, tabs):
    B = hyT.shape[0]
    N1 = tabs["N1"]
    H1 = N1 // 2
    cbn = _hyena_channels(H1 * LANES)
    assert hyT.shape == (B, HY_COLS // cbn, H1 * cbn, LANES) and B % 2 == 0
    ncb = HY_WIDTH // cbn
    hy5 = hyT.reshape(B // 2, 2, HY_COLS // cbn, H1 * cbn, LANES)
    a1 = tabs["a1"].astype(BF16)
    a1i = tabs["a1i"].astype(BF16)
    b2 = tabs["b2"].astype(BF16)
    b2c = tabs["b2c"].astype(BF16)
    twr, twi = tabs["twr"].astype(BF16), tabs["twi"].astype(BF16)
    full = lambda a: pl.BlockSpec(a.shape, lambda c, p: (0,) * a.ndim)
    smem = pl.BlockSpec(memory_space=pltpu.SMEM)
    xspec = lambda k: pl.BlockSpec((None, 2, None, H1 * cbn, LANES), lambda c, p: (p, 0, k * ncb + c, 0, 0))
    kspec = pl.BlockSpec((HY_ORDER, cbn * N1, LANES), lambda c, p: (0, c, 0))
    out = pl.pallas_call(
        functools.partial(_hyena_body, N1=N1, H1=H1, cbn=cbn),
        grid=(ncb, B // 2),
        in_specs=[smem, xspec(0), xspec(1), xspec(2), kspec, kspec,
                  full(a1), full(twr), full(twi), full(b2), full(b2c), full(a1i)],
        out_specs=pl.BlockSpec((None, 2, cbn, H1, LANES), lambda c, p: (p, 0, c, 0, 0)),
        out_shape=jax.ShapeDtypeStruct((B // 2, 2, HY_WIDTH, H1, LANES), F32),
        scratch_shapes=[
            pltpu.VMEM((2 * H1, cbn * LANES), BF16),
            pltpu.VMEM((cbn * N1, 2 * LANES), BF16),
            pltpu.VMEM((2 * N1, cbn * LANES), BF16),
            pltpu.VMEM((2, cbn, H1, LANES), F32),
        ],
        name="hyena_mix",
    )(hyb, hy5, hy5, hy5, kr, ki, a1, twr, twi, b2, b2c, a1i)
    return out.reshape(B, HY_WIDTH, H1, LANES)


GP_ROWS = 6 * ML_HEADS


def _seg_scan(x, op, reverse, fill, T):
    W = x.shape[1]
    pos = lax.broadcasted_iota(jnp.int32, x.shape, 1) % T
    k = 1
    while k < T:
        if reverse:
            x = op(x, jnp.where(pos < T - k, pltpu.roll(x, W - k, 1), fill))
        else:
            x = op(x, jnp.where(pos >= k, pltpu.roll(x, k, 1), fill))
        k *= 2
    return x


def _log_sigmoid(x):
    return jnp.minimum(x, 0.0) - jnp.log(1.0 + jnp.exp(-jnp.abs(x)))


def _gate_prep_body(g_ref, o_ref, *, T):
    g = g_ref[0]
    for d in range(2):
        ig = g[8 * d:8 * d + ML_HEADS]
        fg = g[8 * d + ML_HEADS:8 * d + 2 * ML_HEADS]
        b = _seg_scan(_log_sigmoid(fg), jnp.add, d == 1, 0.0, T)
        a = ig - b
        o_ref[0, 4 * d:4 * d + 4, :] = a
        o_ref[0, 8 + 4 * d:12 + 4 * d, :] = b
        o_ref[0, 16 + 4 * d:20 + 4 * d, :] = _seg_scan(a, jnp.maximum, d == 1, -jnp.inf, T)


def _gate_prep(gT, T):
    B, _, S = gT.shape
    w = min(GATE_W, S)
    return pl.pallas_call(
        functools.partial(_gate_prep_body, T=T),
        grid=(B, S // w),
        in_specs=[pl.BlockSpec((1, N_GATE, w), lambda b, i: (b, 0, i))],
        out_specs=pl.BlockSpec((1, GP_ROWS, w), lambda b, i: (b, 0, i)),
        out_shape=jax.ShapeDtypeStruct((B, GP_ROWS, S), F32),
        name="mlstm_gates",
    )(gT)


def _gp_dir(gp, d, T):
    a = gp[4 * d:4 * d + 4]
    b = gp[8 + 4 * d:12 + 4 * d]
    cm = gp[16 + 4 * d:20 + 4 * d]
    e = 0 if d == 1 else T - 1
    return a, b, cm, b[:, e:e + 1], cm[:, e:e + 1]


def _mlstm_state_body(gf_ref, gb_ref, kf_ref, vf_ref, kb_ref, vb_ref, sf_ref, mf_ref, sb_ref, mb_ref,
                      s_s, m_s, *, T):
    i = pl.program_id(1)

    @pl.when(i == 0)
    def _():
        s_s[...] = jnp.zeros(s_s.shape, F32)
        m_s[...] = jnp.zeros(m_s.shape, F32)

    cps = gf_ref.shape[2] // T
    dirs = ((gf_ref, kf_ref, vf_ref, sf_ref, mf_ref), (gb_ref, kb_ref, vb_ref, sb_ref, mb_ref))
    loc = {}
    for d, (g_ref, k_ref, v_ref, _, _) in enumerate(dirs):
        for u in range(cps):
            ts = slice(u * T, (u + 1) * T)
            a, _, _, btot, amax = _gp_dir(g_ref[0, :, ts], d, T)
            w = jnp.exp(a - amax)
            cs = []
            for h in range(ML_HEADS):
                hs = slice(h * ML_HEAD_DIM, (h + 1) * ML_HEAD_DIM)
                wr = w[h:h + 1, :]
                vw = jnp.concatenate([(v_ref[0, hs, ts].astype(F32) * wr).astype(BF16),
                                      jnp.broadcast_to(wr, (ML_HEAD_DIM, T)).astype(BF16)], axis=0)
                cs.append(jnp.dot(vw, k_ref[0, ts, hs], preferred_element_type=F32))
            loc[d, u] = (btot, amax, cs)
    for d, (_, _, _, s_out, m_out) in enumerate(dirs):
        m_old = m_s[d, 0:ML_HEADS, 0:1]
        sts = [s_s[d, h] for h in range(ML_HEADS)]
        for u in (range(cps) if d == 0 else reversed(range(cps))):
            btot, amax, cs = loc[d, u]
            m_out[0, u] = jnp.broadcast_to(jnp.concatenate([m_old, m_old], axis=0), (SUBLANES, LANES))
            m_new = jnp.maximum(btot + m_old, btot + amax)
            s_old = jnp.exp(btot + m_old - m_new)
            s_loc = jnp.exp(btot + amax - m_new)
            for h in range(ML_HEADS):
                s_out[0, u, h] = sts[h].astype(BF16)
                sts[h] = s_old[h:h + 1, :] * sts[h] + s_loc[h:h + 1, :] * cs[h]
            m_old = m_new
        m_s[d, 0:ML_HEADS, :] = jnp.broadcast_to(m_old, (ML_HEADS, LANES))
        for h in range(ML_HEADS):
            s_s[d, h] = sts[h]


def _mlstm_state(gp, k, vT, T):
    B, S, _ = k.shape
    nc = S // T
    sd = 2 * ML_HEAD_DIM
    cps = ML_STATE_CHUNKS if nc % ML_STATE_CHUNKS == 0 else 1
    ns = nc // cps
    tw = cps * T
    pick = lambda rev, f: (lambda b, i: f(b, ns - 1 - i)) if rev else (lambda b, i: f(b, i))
    gspec = lambda rev: pl.BlockSpec((1, GP_ROWS, tw), pick(rev, lambda b, c: (b, 0, c)))
    kspec = lambda rev: pl.BlockSpec((1, tw, ML_WIDTH), pick(rev, lambda b, c: (b, c, 0)))
    vspec = lambda rev: pl.BlockSpec((1, ML_WIDTH, tw), pick(rev, lambda b, c: (b, 0, c)))
    sspec = lambda rev: pl.BlockSpec((1, cps, ML_HEADS, sd, ML_HEAD_DIM), pick(rev, lambda b, c: (b, c, 0, 0, 0)))
    mspec = lambda rev: pl.BlockSpec((1, cps, SUBLANES, LANES), pick(rev, lambda b, c: (b, c, 0, 0)))
    return pl.pallas_call(
        functools.partial(_mlstm_state_body, T=T),
        grid=(B, ns),
        in_specs=[gspec(False), gspec(True), kspec(False), vspec(False), kspec(True), vspec(True)],
        out_specs=[sspec(False), mspec(False), sspec(True), mspec(True)],
        out_shape=[
            jax.ShapeDtypeStruct((B, nc, ML_HEADS, sd, ML_HEAD_DIM), BF16),
            jax.ShapeDtypeStruct((B, nc, SUBLANES, LANES), F32),
            jax.ShapeDtypeStruct((B, nc, ML_HEADS, sd, ML_HEAD_DIM), BF16),
            jax.ShapeDtypeStruct((B, nc, SUBLANES, LANES), F32),
        ],
        scratch_shapes=[pltpu.VMEM((2, ML_HEADS, sd, ML_HEAD_DIM), F32), pltpu.VMEM((2, SUBLANES, LANES), F32)],
        name="mlstm_state",
    )(gp, gp, k, vT, k, vT)


def _mlstm_out_body(g_ref, q_ref, k_ref, v_ref, o_ref, sf_ref, mf_ref, sb_ref, mb_ref, ng_ref, y_ref, *, T):
    for u in range(g_ref.shape[2] // T):
        _mlstm_out_chunk(u, g_ref, q_ref, k_ref, v_ref, o_ref, sf_ref, mf_ref, sb_ref, mb_ref, ng_ref, y_ref, T)


def _mlstm_out_chunk(u, g_ref, q_ref, k_ref, v_ref, o_ref, sf_ref, mf_ref, sb_ref, mb_ref, ng_ref, y_ref, T):
    ts = slice(u * T, (u + 1) * T)
    gp = g_ref[0, :, ts]
    rows = []
    for d, m_ref in enumerate((mf_ref, mb_ref)):
        a, b, cm, _, _ = _gp_dir(gp, d, T)
        m = m_ref[0, u][0:ML_HEADS, 0:1]
        mt = jnp.maximum(cm, m)
        rows.append((a, mt, jnp.exp(-(b + mt)), jnp.exp(m - mt)))
    a_cols = jnp.concatenate([rows[0][0], rows[1][0],
                              jnp.zeros((LANES - 2 * ML_HEADS, T), F32)], axis=0).T
    ss = lax.broadcasted_iota(jnp.int32, (T, T), 0)
    tt = lax.broadcasted_iota(jnp.int32, (T, T), 1)
    masks = (ss <= tt, ss >= tt)
    ones = jnp.ones((ML_HEAD_DIM, T), BF16)
    hsl = [slice(h * ML_HEAD_DIM, (h + 1) * ML_HEAD_DIM) for h in range(ML_HEADS)]
    ps, qss, v1s = [], [], []
    for h in range(ML_HEADS):
        qT = q_ref[0, hsl[h], ts]
        ps.append(jnp.dot(k_ref[0, ts, hsl[h]], qT, preferred_element_type=F32))
        s_fb = jnp.concatenate([sf_ref[0, u, h], sb_ref[0, u, h]], axis=0)
        qss.append(jnp.dot(s_fb, qT, preferred_element_type=F32))
        v1s.append(jnp.concatenate([v_ref[0, hsl[h], ts], ones], axis=0))
    sms = []
    for h in range(ML_HEADS):
        for d in range(2):
            a_c = a_cols[:, ML_HEADS * d + h:ML_HEADS * d + h + 1]
            e = jnp.exp(jnp.where(masks[d], a_c - rows[d][1][h:h + 1, :], -jnp.inf))
            sms.append((ps[h] * e).astype(BF16))
    accs = [jnp.dot(v1s[h], sms[2 * h + d], preferred_element_type=F32)
            for h in range(ML_HEADS) for d in range(2)]
    sd = 2 * ML_HEAD_DIM
    for h in range(ML_HEADS):
        hm = None
        for d in range(2):
            emt_r = rows[d][2][h:h + 1, :]
            int_r = rows[d][3][h:h + 1, :]
            acc = accs[2 * h + d] + int_r * qss[h][sd * d:sd * (d + 1)]
            den = acc[ML_HEAD_DIM:ML_HEAD_DIM + SUBLANES]
            rden = 1.0 / jnp.maximum(jnp.abs(den), emt_r)
            hd = acc[:ML_HEAD_DIM] * jnp.concatenate([rden] * (ML_HEAD_DIM // SUBLANES), axis=0)
            hm = hd if hm is None else hm + hd
        hn = hm * lax.rsqrt(jnp.mean(hm * hm, axis=0, keepdims=True) + EPS)
        y_ref[0, hsl[h], ts] = (hn * ng_ref[hsl[h], :] * o_ref[0, hsl[h], ts]).astype(BF16)


def _mlstm_out(gp, qT, k, vT, oT, sf, mf, sb, mb, ng, T):
    B, S, _ = k.shape
    nc = S // T
    sd = 2 * ML_HEAD_DIM
    cps = ML_OUT_CHUNKS if nc % ML_OUT_CHUNKS == 0 else 1
    tw = cps * T
    tspec = pl.BlockSpec((1, ML_WIDTH, tw), lambda b, i: (b, 0, i))
    sspec = pl.BlockSpec((1, cps, ML_HEADS, sd, ML_HEAD_DIM), lambda b, i: (b, i, 0, 0, 0))
    mspec = pl.BlockSpec((1, cps, SUBLANES, LANES), lambda b, i: (b, i, 0, 0))
    return pl.pallas_call(
        functools.partial(_mlstm_out_body, T=T),
        grid=(B, nc // cps),
        in_specs=[pl.BlockSpec((1, GP_ROWS, tw), lambda b, i: (b, 0, i)), tspec,
                  pl.BlockSpec((1, tw, ML_WIDTH), lambda b, i: (b, i, 0)), tspec, tspec,
                  sspec, mspec, sspec, mspec, pl.BlockSpec((ML_WIDTH, 1), lambda b, i: (0, 0))],
        out_specs=tspec,
        out_shape=jax.ShapeDtypeStruct((B, ML_WIDTH, S), BF16),
        name="mlstm_out",
    )(gp, qT, k, vT, oT, sf, mf, sb, mb, ng)


def _out_proj_body(x_ref, yhy_ref, yml_ref, why_ref, wml_ref, o_ref, *, tm):
    yh = jnp.swapaxes(yhy_ref[0], 0, 1)
    yt = jnp.concatenate([yh[j].T for j in range(tm // LANES)], axis=0)
    acc = jnp.dot(yt.astype(BF16), why_ref[...], preferred_element_type=F32)
    acc = acc + jnp.dot(yml_ref[0].astype(F32).T.astype(BF16), wml_ref[...], preferred_element_type=F32)
    o_ref[0] = x_ref[0] + acc


def _out_proj(x, yhy, yml, why, wml):
    B, S, D = x.shape
    tm = min(OUT_TM, S)
    full = lambda a: pl.BlockSpec(a.shape, lambda b, i: (0,) * a.ndim)
    return pl.pallas_call(
        functools.partial(_out_proj_body, tm=tm),
        grid=(B, S // tm),
        in_specs=[
            pl.BlockSpec((1, tm, D), lambda b, i: (b, i, 0)),
            pl.BlockSpec((1, HY_WIDTH, tm // LANES, LANES), lambda b, i: (b, 0, i, 0)),
            pl.BlockSpec((1, ML_WIDTH, tm), lambda b, i: (b, 0, i)),
            full(why), full(wml),
        ],
        out_specs=pl.BlockSpec((1, tm, D), lambda b, i: (b, i, 0)),
        out_shape=jax.ShapeDtypeStruct((B, S, D), F32),
        name="out_proj",
    )(x, yhy, yml, why, wml)


def _ffn_body(xp_ref, x_ref, xn_ref, g2_ref, wup_ref, cw_ref, cb_ref, wdn_ref, gf_ref, o_ref, act_s,
              *, tm, ft, final_norm):
    i = pl.program_id(1)
    nt = pl.num_programs(1)
    g2 = g2_ref[...]
    xm = x_ref[0]
    h = _rms(xm, g2)
    hp = jnp.where(i > 0, _rms(xp_ref[0], g2), 0.0)
    hn = jnp.where(i < nt - 1, _rms(xn_ref[0], g2), 0.0)
    hext = jnp.concatenate([hp, h, hn], axis=0).astype(BF16)
    for j in range(D_FF // ft):
        cv = slice(j * ft, (j + 1) * ft)
        cg = slice(D_FF + j * ft, D_FF + (j + 1) * ft)
        uv = jnp.dot(hext, wup_ref[:, cv], preferred_element_type=F32)
        ug = jnp.dot(hext, wup_ref[:, cg], preferred_element_type=F32)
        val = _conv3_rows(uv, cw_ref[:, cv], cb_ref[:, cv], tm)
        gate = _conv3_rows(ug, cw_ref[:, cg], cb_ref[:, cg], tm)
        act_s[:, cv] = (gate * _sigmoid(gate) * val).astype(BF16)
    xo = xm + jnp.dot(act_s[...], wdn_ref[...], preferred_element_type=F32)
    o_ref[0] = _rms(xo, gf_ref[...]) if final_norm else xo


def _ffn(x, g2, wup, cw, cb, wdn, gf, final_norm):
    B, S, D = x.shape
    tm = min(FFN_TM, S)
    nt = S // tm
    r8 = tm // SUBLANES
    full = lambda a: pl.BlockSpec(a.shape, lambda b, i: (0,) * a.ndim)
    once = lambda a: pl.BlockSpec(a.shape, lambda b, i: (0,) * a.ndim, pipeline_mode=pl.Buffered(1))
    return pl.pallas_call(
        functools.partial(_ffn_body, tm=tm, ft=FFN_FT, final_norm=final_norm),
        grid=(B, nt),
        in_specs=[
            pl.BlockSpec((1, SUBLANES, D), lambda b, i: (b, jnp.maximum(i * r8 - 1, 0), 0)),
            pl.BlockSpec((1, tm, D), lambda b, i: (b, i, 0)),
            pl.BlockSpec((1, SUBLANES, D), lambda b, i: (b, jnp.minimum((i + 1) * r8, S // SUBLANES - 1), 0)),
            full(g2), once(wup), full(cw), full(cb), once(wdn), full(gf),
        ],
        out_specs=pl.BlockSpec((1, tm, D), lambda b, i: (b, i, 0)),
        out_shape=jax.ShapeDtypeStruct((B, S, D), F32),
        scratch_shapes=[pltpu.VMEM((tm, D_FF), BF16)],
        name="ffn",
    )(x, x, x, g2, wup, cw, cb, wdn, gf)


def _layer_params(l, norm1_g, w_in, conv_w, conv_b, gate_b, filt_w1, filt_b1, filt_freq1, filt_w2, filt_b2,
                  filt_freq2, filt_w3, hy_bias, ml_norm_g, w_out, norm2_g, w_up, ffn_conv_w, ffn_conv_b, w_down):
    w = w_in[l]
    col = lambda a: a.astype(F32).reshape(-1, 1)
    return dict(
        g1=norm1_g[l].reshape(1, D_MODEL),
        wc=w[:, :CONV_CH].astype(BF16),
        wvo=w[:, OFF_V:].T.astype(BF16),
        cw=conv_w[l], cb=conv_b[l].reshape(1, -1),
        gb=col(gate_b[l]),
        hyb=hy_bias[l].reshape(-1),
        w1T=_split_cols(jnp.pad(filt_w1[l], ((0, HY_HIDDEN - HY_EMB), (0, 0))).T),
        b1=col(filt_b1[l]), f1=col(filt_freq1[l]),
        w2T=_split_cols(filt_w2[l].T), b2=col(filt_b2[l]), f2=col(filt_freq2[l]),
        w3T=_split_cols(filt_w3[l].T).reshape(2, HY_ORDER, HY_WIDTH, 3 * HY_HIDDEN),
        ng=col(ml_norm_g[l]),
        why_o=w_out[l][:HY_WIDTH].astype(BF16), wml_o=w_out[l][HY_WIDTH:].astype(BF16),
        g2=norm2_g[l].reshape(1, D_MODEL),
        wup=w_up[l].astype(BF16), cw_f=ffn_conv_w[l], cb_f=ffn_conv_b[l].reshape(1, -1),
        wdn=w_down[l].astype(BF16),
    )


def _hyena_spectrum(L, p, tabs):
    zT, t_all = _filter_features(L)
    deltas = jnp.asarray(np.abs(np.linspace(HY_MIN_DECAY, HY_MAX_DECAY, HY_WIDTH)).astype(np.float32).reshape(-1, 1))
    kT, sc = _filt(L, zT, t_all, p["w1T"], p["b1"], p["f1"], p["w2T"], p["b2"], p["f2"], p["w3T"], deltas)
    ncb = HY_WIDTH // LANES
    scale = sc.reshape(ncb, HY_ORDER, LANES).transpose(1, 0, 2).reshape(-1)
    return _kfft(scale, kT, tabs)


def _encoder_layer(x, p, tabs, final_g):
    B, S, _ = x.shape
    hyT, gT, q, k, v, o = _in_proj(x, p["g1"], p["wc"], p["wvo"], p["cw"], p["cb"], p["gb"])
    kr, ki = _hyena_spectrum(S, p, tabs)
    y_hy = _hyena(hyT, p["hyb"], kr, ki, tabs)
    T = min(ML_CHUNK, S)
    gp = _gate_prep(gT, T)
    sf, mf, sb, mb = _mlstm_state(gp, k, v, T)
    y_ml = _mlstm_out(gp, q, k, v, o, sf, mf, sb, mb, p["ng"], T)
    xm = _out_proj(x, y_hy, y_ml, p["why_o"], p["wml_o"])
    gf = p["g2"] if final_g is None else final_g.reshape(1, D_MODEL)
    return _ffn(xm, p["g2"], p["wup"], p["cw_f"], p["cb_f"], p["wdn"], gf, final_g is not None)


def _trunk(x, layers, final_g):
    tabs = _dft_tables(x.shape[1])
    for l, p in enumerate(layers):
        x = _encoder_layer(x, p, tabs, final_g if l == len(layers) - 1 else None)
    return x


def kernel(x_prompt, x_sample, norm1_g, w_in, conv_w, conv_b, gate_b, filt_w1, filt_b1, filt_freq1, filt_w2,
           filt_b2, filt_freq2, filt_w3, hy_bias, ml_norm_g, w_out, norm2_g, w_up, ffn_conv_w, ffn_conv_b,
           w_down, final_g):
    weights = (norm1_g, w_in, conv_w, conv_b, gate_b, filt_w1, filt_b1, filt_freq1, filt_w2, filt_b2,
               filt_freq2, filt_w3, hy_bias, ml_norm_g, w_out, norm2_g, w_up, ffn_conv_w, ffn_conv_b, w_down)
    layers = [_layer_params(l, *weights) for l in range(norm1_g.shape[0])]
    return (_trunk(x_prompt, layers, final_g), _trunk(x_sample, layers, final_g))
```

```python
import functools
import math

import jax
import jax.numpy as jnp
import numpy as np
from jax import lax
from jax.experimental import pallas as pl
from jax.experimental.pallas import tpu as pltpu

F32 = jnp.float32
BF16 = jnp.bfloat16

LANES = 128
SUBLANES = 8

D_MODEL = 1024
HY_WIDTH = 512
ML_WIDTH = 512
ML_HEADS = 4
ML_HEAD_DIM = ML_WIDTH // ML_HEADS
HY_ORDER = 2
HY_EMB = 33
HY_BANDS = (HY_EMB - 1) // 2
HY_HIDDEN = 64
HY_DECAY_TARGET = 1e-2
HY_MAX_DECAY = math.log(HY_DECAY_TARGET) / 0.3
HY_MIN_DECAY = math.log(HY_DECAY_TARGET) / 1.5
D_FF = 2816
EPS = 1e-6

HY_COLS = 3 * HY_WIDTH
OFF_Q = HY_COLS
OFF_K = OFF_Q + ML_WIDTH
OFF_V = OFF_K + ML_WIDTH
OFF_O = OFF_V + ML_WIDTH
OFF_G = OFF_O + ML_WIDTH
N_GATE = 4 * ML_HEADS
CONV_CH = OFF_V

ML_CHUNK = 256
ML_OUT_CHUNKS = 8
ML_STATE_CHUNKS = 8
GATE_W = 4096
IN_TM = 1024
CONV_GW = 256
OUT_TM = 1024
FFN_TM = 1024
FFN_FT = 256
FILT_PC = 2048
FFT_ROWS = 2048
HY_PAIRS = 2


def _rms(xv, g):
    ms = jnp.mean(xv * xv, axis=-1, keepdims=True)
    return xv * lax.rsqrt(ms + EPS) * g


def _sigmoid(x):
    return 1.0 / (1.0 + jnp.exp(-x))


def _conv3_rows(p, w, b, tm):
    n = p.shape[0]
    dn = pltpu.roll(p, 1, 0)[SUBLANES:tm + SUBLANES]
    up = pltpu.roll(p, n - 1, 0)[SUBLANES:tm + SUBLANES]
    mid = p[SUBLANES:tm + SUBLANES]
    return dn * w[0:1] + mid * w[1:2] + up * w[2:3] + b


def _in_proj_body(xp_ref, x_ref, xn_ref, g1_ref, wc_ref, wvo_ref, cw_ref, cb_ref, gb_ref,
                  hyT_ref, gT_ref, q_ref, k_ref, v_ref, o_ref, *, tm, cbn):
    i = pl.program_id(1)
    nt = pl.num_programs(1)
    g1 = g1_ref[...]
    h = _rms(x_ref[0], g1)
    hp = jnp.where(i > 0, _rms(xp_ref[0], g1), 0.0)
    hn = jnp.where(i < nt - 1, _rms(xn_ref[0], g1), 0.0)
    hext = jnp.concatenate([hp, h, hn], axis=0).astype(BF16)
    for g in range(CONV_CH // CONV_GW):
        cs = slice(g * CONV_GW, (g + 1) * CONV_GW)
        pc = jnp.dot(hext, wc_ref[:, cs], preferred_element_type=F32)
        c = _conv3_rows(pc, cw_ref[:, cs], cb_ref[:, cs], tm)
        if g < HY_COLS // CONV_GW:
            cT = c.T
            for j in range(tm // LANES):
                for blk in range(CONV_GW // cbn):
                    hyT_ref[0, g * (CONV_GW // cbn) + blk, j * cbn:(j + 1) * cbn, :] = (
                        cT[blk * cbn:(blk + 1) * cbn, j * LANES:(j + 1) * LANES])
        elif g < OFF_K // CONV_GW:
            q_ref[0, g * CONV_GW - OFF_Q:(g + 1) * CONV_GW - OFF_Q, :] = (c * _sigmoid(c)).T.astype(BF16)
        else:
            k_ref[0, :, g * CONV_GW - OFF_K:(g + 1) * CONV_GW - OFF_K] = (
                c * _sigmoid(c) * (ML_HEAD_DIM ** -0.5)).astype(BF16)
    rT = lax.dot_general(wvo_ref[...], h.astype(BF16), (((1,), (1,)), ((), ())),
                         preferred_element_type=F32)
    v_ref[0] = rT[:ML_WIDTH].astype(BF16)
    o_ref[0] = _sigmoid(rT[ML_WIDTH:2 * ML_WIDTH])
    gT_ref[0] = rT[2 * ML_WIDTH:] + gb_ref[...]


def _hyena_channels(S):
    return min(FFT_ROWS // (2 * S // LANES), 32)


def _in_proj(x, g1, wc, wvo, cw, cb, gb):
    B, S, D = x.shape
    cbn = _hyena_channels(S)
    tm = min(IN_TM, S)
    nt = S // tm
    r8 = tm // SUBLANES
    full = lambda shape: pl.BlockSpec(shape, lambda b, i: (0,) * len(shape))
    return pl.pallas_call(
        functools.partial(_in_proj_body, tm=tm, cbn=cbn),
        grid=(B, nt),
        in_specs=[
            pl.BlockSpec((1, SUBLANES, D), lambda b, i: (b, jnp.maximum(i * r8 - 1, 0), 0)),
            pl.BlockSpec((1, tm, D), lambda b, i: (b, i, 0)),
            pl.BlockSpec((1, SUBLANES, D), lambda b, i: (b, jnp.minimum((i + 1) * r8, S // SUBLANES - 1), 0)),
            full((1, D)), full(wc.shape), full(wvo.shape), full(cw.shape), full(cb.shape), full(gb.shape),
        ],
        out_specs=[
            pl.BlockSpec((1, HY_COLS // cbn, (tm // LANES) * cbn, LANES), lambda b, i: (b, 0, i, 0)),
            pl.BlockSpec((1, N_GATE, tm), lambda b, i: (b, 0, i)),
            pl.BlockSpec((1, ML_WIDTH, tm), lambda b, i: (b, 0, i)),
            pl.BlockSpec((1, tm, ML_WIDTH), lambda b, i: (b, i, 0)),
            pl.BlockSpec((1, ML_WIDTH, tm), lambda b, i: (b, 0, i)),
            pl.BlockSpec((1, ML_WIDTH, tm), lambda b, i: (b, 0, i)),
        ],
        out_shape=[
            jax.ShapeDtypeStruct((B, HY_COLS // cbn, (S // LANES) * cbn, LANES), F32),
            jax.ShapeDtypeStruct((B, N_GATE, S), F32),
            jax.ShapeDtypeStruct((B, ML_WIDTH, S), BF16),
            jax.ShapeDtypeStruct((B, S, ML_WIDTH), BF16),
            jax.ShapeDtypeStruct((B, ML_WIDTH, S), BF16),
            jax.ShapeDtypeStruct((B, ML_WIDTH, S), F32),
        ],
        name="in_proj",
    )(x, x, x, g1, wc, wvo, cw, cb, gb)


def _dft_tables(L):
    N = 2 * L
    N1 = N // LANES
    H1 = N1 // 2

    def cs(num, den):
        ang = (2.0 * np.pi / den) * (num % den)
        return np.cos(ang), np.sin(ang)

    k1 = np.arange(N1, dtype=np.int64)[:, None]
    c, s = cs(k1 * np.arange(H1, dtype=np.int64)[None, :], N1)
    a1 = np.block([[c, s], [-s, c]])
    a1i = np.block([[c.T, -s.T], [s.T, c.T]]) * (1.0 / N)
    cf, sf = cs(k1 * np.arange(N1, dtype=np.int64)[None, :], N1)
    a1k = np.concatenate([cf, -sf], axis=0)
    ct, st = cs(k1 * np.arange(LANES, dtype=np.int64)[None, :], N)
    n2 = np.arange(LANES, dtype=np.int64)
    cg, sg = cs(n2[:, None] * n2[None, :], LANES)
    b2 = np.block([[cg, -sg], [sg, cg]])
    b2c = np.block([[cg, sg], [-sg, cg]])
    tabs = dict(a1=a1, a1i=a1i, a1k=a1k, twr=ct, twi=-st, b2=b2, b2c=b2c)
    tabs = {k: jnp.asarray(v.astype(np.float32)) for k, v in tabs.items()}
    return dict(N1=N1, H1=H1, **tabs)


def _filter_features(L):
    t = np.linspace(0.0, 1.0, L)[:, None]
    freqs = np.linspace(1e-4, HY_BANDS - 1, HY_BANDS)
    ang = (2.0 * np.pi / L) * np.arange(L, dtype=np.float64)[:, None] * freqs[None, :]
    z = np.concatenate([t, np.cos(ang), -np.sin(ang)], axis=-1)
    z_all = np.concatenate([z, z[0:1], z[L - 1:0:-1]], axis=0)
    t_all = np.concatenate([t, t[0:1], t[L - 1:0:-1]], axis=0)
    zT = np.pad(z_all, ((0, 0), (0, HY_HIDDEN - HY_EMB))).T
    return jnp.asarray(zT.astype(np.float32)), jnp.asarray(t_all.T.astype(np.float32))


def _split_bf16(x):
    hi = x.astype(BF16)
    return hi, (x - hi.astype(F32)).astype(BF16)


def _dot_split(w3, x):
    hi, lo = _split_bf16(x)
    return jnp.dot(w3, jnp.concatenate([hi, lo, hi], axis=0), preferred_element_type=F32)


def _split_cols(w):
    hi, lo = _split_bf16(w)
    return jnp.concatenate([hi, hi, lo], axis=-1)


def _filt_mlp_body(zT_ref, w1T_ref, b1_ref, f1_ref, w2T_ref, b2_ref, f2_ref, h2_ref):
    h1 = jnp.sin(f1_ref[...] * (_dot_split(w1T_ref[...], zT_ref[...]) + b1_ref[...]))
    h2_ref[...] = jnp.sin(f2_ref[...] * (_dot_split(w2T_ref[...], h1) + b2_ref[...]))


def _filt_body(h2_ref, t_ref, w3T_ref, dl_ref, kT_ref, sc_ref, *, L, pc):
    p = pl.program_id(1)
    last = pl.num_programs(1) - 1
    w3 = w3T_ref[0].reshape(HY_ORDER * LANES, 3 * HY_HIDDEN)
    h3 = _dot_split(w3, h2_ref[...])
    win = jnp.exp(-t_ref[...] * dl_ref[...])
    n_idx = p * pc + lax.broadcasted_iota(jnp.int32, (1, pc), 1)
    win = jnp.where(n_idx == L, 0.0, win)
    kern = h3 * jnp.concatenate([win] * HY_ORDER, axis=0)
    for o in range(HY_ORDER):
        for j in range(pc // LANES):
            kT_ref[o, j] = kern[o * LANES:(o + 1) * LANES, j * LANES:(j + 1) * LANES]
    ss = jnp.sum(kern * kern, axis=1, keepdims=True)

    @pl.when(p == 0)
    def _():
        sc_ref[0] = ss

    @pl.when(jnp.logical_and(p > 0, p < last))
    def _():
        sc_ref[0] = sc_ref[0] + ss

    @pl.when(p == last)
    def _():
        sc_ref[0] = lax.rsqrt(sc_ref[0] + ss)


def _filt(L, zT, t_all, w1T, b1, f1, w2T, b2, f2, w3T, dl):
    pc = min(FILT_PC, L)
    npc = 2 * L // pc
    ncb = HY_WIDTH // LANES
    full1 = lambda a: pl.BlockSpec(a.shape, lambda p: (0,) * a.ndim)
    h2T = pl.pallas_call(
        _filt_mlp_body,
        grid=(npc,),
        in_specs=[pl.BlockSpec((HY_HIDDEN, pc), lambda p: (0, p)),
                  full1(w1T), full1(b1), full1(f1), full1(w2T), full1(b2), full1(f2)],
        out_specs=pl.BlockSpec((HY_HIDDEN, pc), lambda p: (0, p)),
        out_shape=jax.ShapeDtypeStruct((HY_HIDDEN, 2 * L), F32),
        name="hyena_filter_mlp",
    )(zT, w1T, b1, f1, w2T, b2, f2)
    return pl.pallas_call(
        functools.partial(_filt_body, L=L, pc=pc),
        grid=(ncb, npc),
        in_specs=[
            pl.BlockSpec((HY_HIDDEN, pc), lambda c, p: (0, p)),
            pl.BlockSpec((1, pc), lambda c, p: (0, p)),
            pl.BlockSpec((1, HY_ORDER, LANES, 3 * HY_HIDDEN), lambda c, p: ((p * pc) // L, 0, c, 0)),
            pl.BlockSpec((LANES, 1), lambda c, p: (c, 0)),
        ],
        out_specs=[
            pl.BlockSpec((HY_ORDER, pc // LANES, LANES, LANES), lambda c, p: (0, p, c, 0)),
            pl.BlockSpec((1, HY_ORDER * LANES, 1), lambda c, p: (c, 0, 0)),
        ],
        out_shape=[
            jax.ShapeDtypeStruct((HY_ORDER, 2 * L // LANES, HY_WIDTH, LANES), F32),
            jax.ShapeDtypeStruct((ncb, HY_ORDER * LANES, 1), F32),
        ],
        name="hyena_filter",
    )(h2T, t_all, w3T, dl)


def _kfft_body(sc_ref, kT_ref, a1k_ref, twr_ref, twi_ref, b2_ref, kr_ref, ki_ref, x_s, l2_s, *, N1, cbn):
    o = pl.program_id(0)
    cb = pl.program_id(1)
    ks = jnp.swapaxes(kT_ref[...], 0, 1)
    for c in range(cbn):
        hi, lo = _split_bf16(ks[c] * sc_ref[o * HY_WIDTH + cb * cbn + c])
        cs = slice(c * LANES, (c + 1) * LANES)
        x_s[0:N1, cs] = hi
        x_s[N1:2 * N1, cs] = lo
        x_s[2 * N1:3 * N1, cs] = hi
    y = jnp.dot(a1k_ref[...], x_s[...], preferred_element_type=F32)
    twr = twr_ref[...]
    twi = twi_ref[...]
    for c in range(cbn):
        yr = y[0:N1, c * LANES:(c + 1) * LANES]
        yi = y[N1:2 * N1, c * LANES:(c + 1) * LANES]
        rs = slice(c * N1, (c + 1) * N1)
        for part, val in ((0, yr * twr - yi * twi), (1, yr * twi + yi * twr)):
            hi, lo = _split_bf16(val)
            l2_s[rs, part * LANES:(part + 1) * LANES] = hi
            l2_s[rs, (2 + part) * LANES:(3 + part) * LANES] = lo
            l2_s[rs, (4 + part) * LANES:(5 + part) * LANES] = hi
    z = jnp.dot(l2_s[...], b2_ref[...], preferred_element_type=F32)
    kr_ref[...] = z[:, :LANES].astype(BF16)
    ki_ref[...] = z[:, LANES:].astype(BF16)


def _kfft(scale, kT, tabs):
    N1 = tabs["N1"]
    cbn = _hyena_channels(N1 * LANES // 2)
    ncb = HY_WIDTH // cbn
    full = lambda a: pl.BlockSpec(a.shape, lambda o, c: (0,) * a.ndim)
    twr, twi = tabs["twr"], tabs["twi"]
    a_hi, a_lo = _split_bf16(tabs["a1k"])
    b_hi, b_lo = _split_bf16(tabs["b2"])
    a1k = jnp.concatenate([a_hi, a_hi, a_lo], axis=1)
    b2 = jnp.concatenate([b_hi, b_hi, b_lo], axis=0)
    return pl.pallas_call(
        functools.partial(_kfft_body, N1=N1, cbn=cbn),
        grid=(HY_ORDER, ncb),
        in_specs=[
            pl.BlockSpec(memory_space=pltpu.SMEM),
            pl.BlockSpec((None, N1, cbn, LANES), lambda o, c: (o, 0, c, 0)),
            full(a1k), full(twr), full(twi), full(b2),
        ],
        out_specs=[
            pl.BlockSpec((None, cbn * N1, LANES), lambda o, c: (o, c, 0)),
            pl.BlockSpec((None, cbn * N1, LANES), lambda o, c: (o, c, 0)),
        ],
        out_shape=[
            jax.ShapeDtypeStruct((HY_ORDER, HY_WIDTH * N1, LANES), BF16),
            jax.ShapeDtypeStruct((HY_ORDER, HY_WIDTH * N1, LANES), BF16),
        ],
        scratch_shapes=[pltpu.VMEM((3 * N1, cbn * LANES), BF16), pltpu.VMEM((cbn * N1, 6 * LANES), BF16)],
        name="hyena_kfft",
    )(scale, kT, a1k, twr, twi, b2)


def _hyena_body(hyb_ref, x1_ref, x2_ref, v_ref, kr_ref, ki_ref, a1_ref, twr_ref, twi_ref,
                b2_ref, b2c_ref, a1i_ref, o_ref, x_s, l2_s, r1_s, z_s, *, N1, H1, cbn):
    cb = pl.program_id(0)

    def slabs(ref, pi, e):
        return jnp.swapaxes(ref[pi, e].reshape(H1, cbn, LANES), 0, 1)

    def long_conv(order):
        twr = twr_ref[...]
        twi = twi_ref[...]
        y = jnp.dot(a1_ref[...], x_s[...], preferred_element_type=F32).astype(BF16)
        for c in range(cbn):
            yr = y[0:N1, c * LANES:(c + 1) * LANES]
            yi = y[N1:2 * N1, c * LANES:(c + 1) * LANES]
            l2_s[c * N1:(c + 1) * N1, 0:LANES] = yr * twr - yi * twi
            l2_s[c * N1:(c + 1) * N1, LANES:2 * LANES] = yr * twi + yi * twr
        z = jnp.dot(l2_s[...], b2_ref[...], preferred_element_type=F32).astype(BF16)
        zr = z[:, :LANES]
        zi = z[:, LANES:]
        kr = kr_ref[order]
        ki = ki_ref[order]
        l2_s[:, 0:LANES] = zr * kr - zi * ki
        l2_s[:, LANES:2 * LANES] = zr * ki + zi * kr
        v = jnp.dot(l2_s[...], b2c_ref[...], preferred_element_type=F32).astype(BF16)
        for c in range(cbn):
            vr = v[c * N1:(c + 1) * N1, :LANES]
            vi = v[c * N1:(c + 1) * N1, LANES:]
            r1_s[0:N1, c * LANES:(c + 1) * LANES] = vr * twr + vi * twi
            r1_s[N1:2 * N1, c * LANES:(c + 1) * LANES] = vi * twr - vr * twi
        return jnp.dot(a1i_ref[...], r1_s[...], preferred_element_type=F32)

    for pi in range(o_ref.shape[0]):
        for e in range(2):
            vs = slabs(v_ref, pi, e)
            z_s[e] = vs
            for c in range(cbn):
                x_s[e * H1:(e + 1) * H1, c * LANES:(c + 1) * LANES] = vs[c].astype(BF16)
        y1 = long_conv(0)
        for e in range(2):
            x1s = slabs(x1_ref, pi, e)
            for c in range(cbn):
                c1 = y1[e * H1:(e + 1) * H1, c * LANES:(c + 1) * LANES] + hyb_ref[cb * cbn + c] * z_s[e, c]
                z2 = x1s[c] * c1
                z_s[e, c] = z2
                x_s[e * H1:(e + 1) * H1, c * LANES:(c + 1) * LANES] = z2.astype(BF16)
        y2 = long_conv(1)
        for e in range(2):
            x2s = slabs(x2_ref, pi, e)
            for c in range(cbn):
                c2 = (y2[e * H1:(e + 1) * H1, c * LANES:(c + 1) * LANES]
                      + hyb_ref[HY_WIDTH + cb * cbn + c] * z_s[e, c])
                o_ref[pi, e, c] = x2s[c] * c2


def _hyena(hyT, hyb, kr, ki, tabs):
    B = hyT.shape[0]
    N1 = tabs["N1"]
    H1 = N1 // 2
    cbn = _hyena_channels(H1 * LANES)
    assert hyT.shape == (B, HY_COLS // cbn, H1 * cbn, LANES) and B % 2 == 0
    ncb = HY_WIDTH // cbn
    hy5 = hyT.reshape(B // 2, 2, HY_COLS // cbn, H1 * cbn, LANES)
    a1 = tabs["a1"].astype(BF16)
    a1i = tabs["a1i"].astype(BF16)
    b2 = tabs["b2"].astype(BF16)
    b2c = tabs["b2c"].astype(BF16)
    twr, twi = tabs["twr"].astype(BF16), tabs["twi"].astype(BF16)
    full = lambda a: pl.BlockSpec(a.shape, lambda c, p: (0,) * a.ndim)
    smem = pl.BlockSpec(memory_space=pltpu.SMEM)
    npp = HY_PAIRS if (B // 2) % HY_PAIRS == 0 else 1
    xspec = lambda k: pl.BlockSpec((npp, 2, None, H1 * cbn, LANES), lambda c, p: (p, 0, k * ncb + c, 0, 0))
    kspec = pl.BlockSpec((HY_ORDER, cbn * N1, LANES), lambda c, p: (0, c, 0))
    out = pl.pallas_call(
        functools.partial(_hyena_body, N1=N1, H1=H1, cbn=cbn),
        grid=(ncb, B // 2 // npp),
        in_specs=[smem, xspec(0), xspec(1), xspec(2), kspec, kspec,
                  full(a1), full(twr), full(twi), full(b2), full(b2c), full(a1i)],
        out_specs=pl.BlockSpec((npp, 2, cbn, H1, LANES), lambda c, p: (p, 0, c, 0, 0)),
        out_shape=jax.ShapeDtypeStruct((B // 2, 2, HY_WIDTH, H1, LANES), F32),
        scratch_shapes=[
            pltpu.VMEM((2 * H1, cbn * LANES), BF16),
            pltpu.VMEM((cbn * N1, 2 * LANES), BF16),
            pltpu.VMEM((2 * N1, cbn * LANES), BF16),
            pltpu.VMEM((2, cbn, H1, LANES), F32),
        ],
        name="hyena_mix",
    )(hyb, hy5, hy5, hy5, kr, ki, a1, twr, twi, b2, b2c, a1i)
    return out.reshape(B, HY_WIDTH, H1, LANES)


GP_ROWS = 6 * ML_HEADS


def _seg_scan(x, op, reverse, fill, T):
    W = x.shape[1]
    pos = lax.broadcasted_iota(jnp.int32, x.shape, 1) % T
    k = 1
    while k < T:
        if reverse:
            x = op(x, jnp.where(pos < T - k, pltpu.roll(x, W - k, 1), fill))
        else:
            x = op(x, jnp.where(pos >= k, pltpu.roll(x, k, 1), fill))
        k *= 2
    return x


def _log_sigmoid(x):
    return jnp.minimum(x, 0.0) - jnp.log(1.0 + jnp.exp(-jnp.abs(x)))


def _gate_prep_body(g_ref, o_ref, *, T):
    g = g_ref[0]
    for d in range(2):
        ig = g[8 * d:8 * d + ML_HEADS]
        fg = g[8 * d + ML_HEADS:8 * d + 2 * ML_HEADS]
        b = _seg_scan(_log_sigmoid(fg), jnp.add, d == 1, 0.0, T)
        a = ig - b
        o_ref[0, 4 * d:4 * d + 4, :] = a
        o_ref[0, 8 + 4 * d:12 + 4 * d, :] = b
        o_ref[0, 16 + 4 * d:20 + 4 * d, :] = _seg_scan(a, jnp.maximum, d == 1, -jnp.inf, T)


def _gate_prep(gT, T):
    B, _, S = gT.shape
    w = min(GATE_W, S)
    return pl.pallas_call(
        functools.partial(_gate_prep_body, T=T),
        grid=(B, S // w),
        in_specs=[pl.BlockSpec((1, N_GATE, w), lambda b, i: (b, 0, i))],
        out_specs=pl.BlockSpec((1, GP_ROWS, w), lambda b, i: (b, 0, i)),
        out_shape=jax.ShapeDtypeStruct((B, GP_ROWS, S), F32),
        name="mlstm_gates",
    )(gT)


def _gp_dir(gp, d, T):
    a = gp[4 * d:4 * d + 4]
    b = gp[8 + 4 * d:12 + 4 * d]
    cm = gp[16 + 4 * d:20 + 4 * d]
    e = 0 if d == 1 else T - 1
    return a, b, cm, b[:, e:e + 1], cm[:, e:e + 1]


def _mlstm_state_body(gf_ref, gb_ref, kf_ref, vf_ref, kb_ref, vb_ref, sf_ref, mf_ref, sb_ref, mb_ref,
                      s_s, m_s, *, T):
    i = pl.program_id(1)

    @pl.when(i == 0)
    def _():
        s_s[...] = jnp.zeros(s_s.shape, F32)
        m_s[...] = jnp.zeros(m_s.shape, F32)

    cps = gf_ref.shape[2] // T
    dirs = ((gf_ref, kf_ref, vf_ref, sf_ref, mf_ref), (gb_ref, kb_ref, vb_ref, sb_ref, mb_ref))
    loc = {}
    for d, (g_ref, k_ref, v_ref, _, _) in enumerate(dirs):
        for u in range(cps):
            ts = slice(u * T, (u + 1) * T)
            a, _, _, btot, amax = _gp_dir(g_ref[0, :, ts], d, T)
            w = jnp.exp(a - amax)
            cs = []
            for h in range(ML_HEADS):
                hs = slice(h * ML_HEAD_DIM, (h + 1) * ML_HEAD_DIM)
                wr = w[h:h + 1, :]
                vw = jnp.concatenate([(v_ref[0, hs, ts].astype(F32) * wr).astype(BF16),
                                      jnp.broadcast_to(wr, (ML_HEAD_DIM, T)).astype(BF16)], axis=0)
                cs.append(jnp.dot(vw, k_ref[0, ts, hs], preferred_element_type=F32))
            loc[d, u] = (btot, amax, cs)
    for d, (_, _, _, s_out, m_out) in enumerate(dirs):
        m_old = m_s[d, 0:ML_HEADS, 0:1]
        sts = [s_s[d, h] for h in range(ML_HEADS)]
        for u in (range(cps) if d == 0 else reversed(range(cps))):
            btot, amax, cs = loc[d, u]
            m_out[0, u] = jnp.broadcast_to(jnp.concatenate([m_old, m_old], axis=0), (SUBLANES, LANES))
            m_new = jnp.maximum(btot + m_old, btot + amax)
            s_old = jnp.exp(btot + m_old - m_new)
            s_loc = jnp.exp(btot + amax - m_new)
            for h in range(ML_HEADS):
                s_out[0, u, h] = sts[h].astype(BF16)
                sts[h] = s_old[h:h + 1, :] * sts[h] + s_loc[h:h + 1, :] * cs[h]
            m_old = m_new
        m_s[d, 0:ML_HEADS, :] = jnp.broadcast_to(m_old, (ML_HEADS, LANES))
        for h in range(ML_HEADS):
            s_s[d, h] = sts[h]


def _mlstm_state(gp, k, vT, T):
    B, S, _ = k.shape
    nc = S // T
    sd = 2 * ML_HEAD_DIM
    cps = ML_STATE_CHUNKS if nc % ML_STATE_CHUNKS == 0 else 1
    ns = nc // cps
    tw = cps * T
    pick = lambda rev, f: (lambda b, i: f(b, ns - 1 - i)) if rev else (lambda b, i: f(b, i))
    gspec = lambda rev: pl.BlockSpec((1, GP_ROWS, tw), pick(rev, lambda b, c: (b, 0, c)))
    kspec = lambda rev: pl.BlockSpec((1, tw, ML_WIDTH), pick(rev, lambda b, c: (b, c, 0)))
    vspec = lambda rev: pl.BlockSpec((1, ML_WIDTH, tw), pick(rev, lambda b, c: (b, 0, c)))
    sspec = lambda rev: pl.BlockSpec((1, cps, ML_HEADS, sd, ML_HEAD_DIM), pick(rev, lambda b, c: (b, c, 0, 0, 0)))
    mspec = lambda rev: pl.BlockSpec((1, cps, SUBLANES, LANES), pick(rev, lambda b, c: (b, c, 0, 0)))
    return pl.pallas_call(
        functools.partial(_mlstm_state_body, T=T),
        grid=(B, ns),
        in_specs=[gspec(False), gspec(True), kspec(False), vspec(False), kspec(True), vspec(True)],
        out_specs=[sspec(False), mspec(False), sspec(True), mspec(True)],
        out_shape=[
            jax.ShapeDtypeStruct((B, nc, ML_HEADS, sd, ML_HEAD_DIM), BF16),
            jax.ShapeDtypeStruct((B, nc, SUBLANES, LANES), F32),
            jax.ShapeDtypeStruct((B, nc, ML_HEADS, sd, ML_HEAD_DIM), BF16),
            jax.ShapeDtypeStruct((B, nc, SUBLANES, LANES), F32),
        ],
        scratch_shapes=[pltpu.VMEM((2, ML_HEADS, sd, ML_HEAD_DIM), F32), pltpu.VMEM((2, SUBLANES, LANES), F32)],
        name="mlstm_state",
    )(gp, gp, k, vT, k, vT)


def _mlstm_out_body(g_ref, q_ref, k_ref, v_ref, o_ref, sf_ref, mf_ref, sb_ref, mb_ref, ng_ref, y_ref, *, T):
    for u in range(g_ref.shape[2] // T):
        _mlstm_out_chunk(u, g_ref, q_ref, k_ref, v_ref, o_ref, sf_ref, mf_ref, sb_ref, mb_ref, ng_ref, y_ref, T)


def _mlstm_out_chunk(u, g_ref, q_ref, k_ref, v_ref, o_ref, sf_ref, mf_ref, sb_ref, mb_ref, ng_ref, y_ref, T):
    ts = slice(u * T, (u + 1) * T)
    gp = g_ref[0, :, ts]
    rows = []
    for d, m_ref in enumerate((mf_ref, mb_ref)):
        a, b, cm, _, _ = _gp_dir(gp, d, T)
        m = m_ref[0, u][0:ML_HEADS, 0:1]
        mt = jnp.maximum(cm, m)
        rows.append((a, mt, jnp.exp(-(b + mt)), jnp.exp(m - mt)))
    a_cols = jnp.concatenate([rows[0][0], rows[1][0],
                              jnp.zeros((LANES - 2 * ML_HEADS, T), F32)], axis=0).T
    ss = lax.broadcasted_iota(jnp.int32, (T, T), 0)
    tt = lax.broadcasted_iota(jnp.int32, (T, T), 1)
    masks = (ss <= tt, ss >= tt)
    ones = jnp.ones((ML_HEAD_DIM, T), BF16)
    hsl = [slice(h * ML_HEAD_DIM, (h + 1) * ML_HEAD_DIM) for h in range(ML_HEADS)]
    ps, qss, v1s = [], [], []
    for h in range(ML_HEADS):
        qT = q_ref[0, hsl[h], ts]
        ps.append(jnp.dot(k_ref[0, ts, hsl[h]], qT, preferred_element_type=F32))
        s_fb = jnp.concatenate([sf_ref[0, u, h], sb_ref[0, u, h]], axis=0)
        qss.append(jnp.dot(s_fb, qT, preferred_element_type=F32))
        v1s.append(jnp.concatenate([v_ref[0, hsl[h], ts], ones], axis=0))
    sms = []
    for h in range(ML_HEADS):
        for d in range(2):
            a_c = a_cols[:, ML_HEADS * d + h:ML_HEADS * d + h + 1]
            e = jnp.exp(jnp.where(masks[d], a_c - rows[d][1][h:h + 1, :], -jnp.inf))
            sms.append((ps[h] * e).astype(BF16))
    accs = [jnp.dot(v1s[h], sms[2 * h + d], preferred_element_type=F32)
            for h in range(ML_HEADS) for d in range(2)]
    sd = 2 * ML_HEAD_DIM
    for h in range(ML_HEADS):
        hm = None
        for d in range(2):
            emt_r = rows[d][2][h:h + 1, :]
            int_r = rows[d][3][h:h + 1, :]
            acc = accs[2 * h + d] + int_r * qss[h][sd * d:sd * (d + 1)]
            den = acc[ML_HEAD_DIM:ML_HEAD_DIM + SUBLANES]
            rden = 1.0 / jnp.maximum(jnp.abs(den), emt_r)
            hd = acc[:ML_HEAD_DIM] * jnp.concatenate([rden] * (ML_HEAD_DIM // SUBLANES), axis=0)
            hm = hd if hm is None else hm + hd
        hn = hm * lax.rsqrt(jnp.mean(hm * hm, axis=0, keepdims=True) + EPS)
        y_ref[0, hsl[h], ts] = (hn * ng_ref[hsl[h], :] * o_ref[0, hsl[h], ts]).astype(BF16)


def _mlstm_out(gp, qT, k, vT, oT, sf, mf, sb, mb, ng, T):
    B, S, _ = k.shape
    nc = S // T
    sd = 2 * ML_HEAD_DIM
    cps = ML_OUT_CHUNKS if nc % ML_OUT_CHUNKS == 0 else 1
    tw = cps * T
    tspec = pl.BlockSpec((1, ML_WIDTH, tw), lambda b, i: (b, 0, i))
    sspec = pl.BlockSpec((1, cps, ML_HEADS, sd, ML_HEAD_DIM), lambda b, i: (b, i, 0, 0, 0))
    mspec = pl.BlockSpec((1, cps, SUBLANES, LANES), lambda b, i: (b, i, 0, 0))
    return pl.pallas_call(
        functools.partial(_mlstm_out_body, T=T),
        grid=(B, nc // cps),
        in_specs=[pl.BlockSpec((1, GP_ROWS, tw), lambda b, i: (b, 0, i)), tspec,
                  pl.BlockSpec((1, tw, ML_WIDTH), lambda b, i: (b, i, 0)), tspec, tspec,
                  sspec, mspec, sspec, mspec, pl.BlockSpec((ML_WIDTH, 1), lambda b, i: (0, 0))],
        out_specs=tspec,
        out_shape=jax.ShapeDtypeStruct((B, ML_WIDTH, S), BF16),
        name="mlstm_out",
    )(gp, qT, k, vT, oT, sf, mf, sb, mb, ng)


def _out_proj_body(x_ref, yhy_ref, yml_ref, why_ref, wml_ref, o_ref, *, tm):
    yh = jnp.swapaxes(yhy_ref[0], 0, 1)
    yt = jnp.concatenate([yh[j].T for j in range(tm // LANES)], axis=0)
    acc = jnp.dot(yt.astype(BF16), why_ref[...], preferred_element_type=F32)
    acc = acc + jnp.dot(yml_ref[0].astype(F32).T.astype(BF16), wml_ref[...], preferred_element_type=F32)
    o_ref[0] = x_ref[0] + acc


def _out_proj(x, yhy, yml, why, wml):
    B, S, D = x.shape
    tm = min(OUT_TM, S)
    full = lambda a: pl.BlockSpec(a.shape, lambda b, i: (0,) * a.ndim)
    return pl.pallas_call(
        functools.partial(_out_proj_body, tm=tm),
        grid=(B, S // tm),
        in_specs=[
            pl.BlockSpec((1, tm, D), lambda b, i: (b, i, 0)),
            pl.BlockSpec((1, HY_WIDTH, tm // LANES, LANES), lambda b, i: (b, 0, i, 0)),
            pl.BlockSpec((1, ML_WIDTH, tm), lambda b, i: (b, 0, i)),
            full(why), full(wml),
        ],
        out_specs=pl.BlockSpec((1, tm, D), lambda b, i: (b, i, 0)),
        out_shape=jax.ShapeDtypeStruct((B, S, D), F32),
        name="out_proj",
    )(x, yhy, yml, why, wml)


def _ffn_body(xp_ref, x_ref, xn_ref, g2_ref, wup_ref, cw_ref, cb_ref, wdn_ref, gf_ref, o_ref, act_s,
              *, tm, ft, final_norm):
    i = pl.program_id(1)
    nt = pl.num_programs(1)
    g2 = g2_ref[...]
    xm = x_ref[0]
    h = _rms(xm, g2)
    hp = jnp.where(i > 0, _rms(xp_ref[0], g2), 0.0)
    hn = jnp.where(i < nt - 1, _rms(xn_ref[0], g2), 0.0)
    hext = jnp.concatenate([hp, h, hn], axis=0).astype(BF16)
    for j in range(D_FF // ft):
        cv = slice(j * ft, (j + 1) * ft)
        cg = slice(D_FF + j * ft, D_FF + (j + 1) * ft)
        uv = jnp.dot(hext, wup_ref[:, cv], preferred_element_type=F32)
        ug = jnp.dot(hext, wup_ref[:, cg], preferred_element_type=F32)
        val = _conv3_rows(uv, cw_ref[:, cv], cb_ref[:, cv], tm)
        gate = _conv3_rows(ug, cw_ref[:, cg], cb_ref[:, cg], tm)
        act_s[:, cv] = (gate * _sigmoid(gate) * val).astype(BF16)
    hh = tm // 2
    for r in range(2):
        rs = slice(r * hh, (r + 1) * hh)
        xo = xm[rs] + jnp.dot(act_s[rs, :], wdn_ref[...], preferred_element_type=F32)
        o_ref[0, rs, :] = _rms(xo, gf_ref[...]) if final_norm else xo


def _ffn(x, g2, wup, cw, cb, wdn, gf, final_norm):
    B, S, D = x.shape
    tm = min(FFN_TM, S)
    nt = S // tm
    r8 = tm // SUBLANES
    full = lambda a: pl.BlockSpec(a.shape, lambda b, i: (0,) * a.ndim)
    once = lambda a: pl.BlockSpec(a.shape, lambda b, i: (0,) * a.ndim, pipeline_mode=pl.Buffered(1))
    return pl.pallas_call(
        functools.partial(_ffn_body, tm=tm, ft=FFN_FT, final_norm=final_norm),
        grid=(B, nt),
        in_specs=[
            pl.BlockSpec((1, SUBLANES, D), lambda b, i: (b, jnp.maximum(i * r8 - 1, 0), 0)),
            pl.BlockSpec((1, tm, D), lambda b, i: (b, i, 0)),
            pl.BlockSpec((1, SUBLANES, D), lambda b, i: (b, jnp.minimum((i + 1) * r8, S // SUBLANES - 1), 0)),
            full(g2), once(wup), full(cw), full(cb), once(wdn), full(gf),
        ],
        out_specs=pl.BlockSpec((1, tm, D), lambda b, i: (b, i, 0)),
        out_shape=jax.ShapeDtypeStruct((B, S, D), F32),
        scratch_shapes=[pltpu.VMEM((tm, D_FF), BF16)],
        name="ffn",
    )(x, x, x, g2, wup, cw, cb, wdn, gf)


def _layer_params(l, norm1_g, w_in, conv_w, conv_b, gate_b, filt_w1, filt_b1, filt_freq1, filt_w2, filt_b2,
                  filt_freq2, filt_w3, hy_bias, ml_norm_g, w_out, norm2_g, w_up, ffn_conv_w, ffn_conv_b, w_down):
    w = w_in[l]
    col = lambda a: a.astype(F32).reshape(-1, 1)
    return dict(
        g1=norm1_g[l].reshape(1, D_MODEL),
        wc=w[:, :CONV_CH].astype(BF16),
        wvo=w[:, OFF_V:].T.astype(BF16),
        cw=conv_w[l], cb=conv_b[l].reshape(1, -1),
        gb=col(gate_b[l]),
        hyb=hy_bias[l].reshape(-1),
        w1T=_split_cols(jnp.pad(filt_w1[l], ((0, HY_HIDDEN - HY_EMB), (0, 0))).T),
        b1=col(filt_b1[l]), f1=col(filt_freq1[l]),
        w2T=_split_cols(filt_w2[l].T), b2=col(filt_b2[l]), f2=col(filt_freq2[l]),
        w3T=_split_cols(filt_w3[l].T).reshape(2, HY_ORDER, HY_WIDTH, 3 * HY_HIDDEN),
        ng=col(ml_norm_g[l]),
        why_o=w_out[l][:HY_WIDTH].astype(BF16), wml_o=w_out[l][HY_WIDTH:].astype(BF16),
        g2=norm2_g[l].reshape(1, D_MODEL),
        wup=w_up[l].astype(BF16), cw_f=ffn_conv_w[l], cb_f=ffn_conv_b[l].reshape(1, -1),
        wdn=w_down[l].astype(BF16),
    )


def _hyena_spectrum(L, p, tabs):
    zT, t_all = _filter_features(L)
    deltas = jnp.asarray(np.abs(np.linspace(HY_MIN_DECAY, HY_MAX_DECAY, HY_WIDTH)).astype(np.float32).reshape(-1, 1))
    kT, sc = _filt(L, zT, t_all, p["w1T"], p["b1"], p["f1"], p["w2T"], p["b2"], p["f2"], p["w3T"], deltas)
    ncb = HY_WIDTH // LANES
    scale = sc.reshape(ncb, HY_ORDER, LANES).transpose(1, 0, 2).reshape(-1)
    return _kfft(scale, kT, tabs)


def _encoder_layer(x, p, tabs, final_g):
    B, S, _ = x.shape
    hyT, gT, q, k, v, o = _in_proj(x, p["g1"], p["wc"], p["wvo"], p["cw"], p["cb"], p["gb"])
    kr, ki = _hyena_spectrum(S, p, tabs)
    y_hy = _hyena(hyT, p["hyb"], kr, ki, tabs)
    T = min(ML_CHUNK, S)
    gp = _gate_prep(gT, T)
    sf, mf, sb, mb = _mlstm_state(gp, k, v, T)
    y_ml = _mlstm_out(gp, q, k, v, o, sf, mf, sb, mb, p["ng"], T)
    xm = _out_proj(x, y_hy, y_ml, p["why_o"], p["wml_o"])
    gf = p["g2"] if final_g is None else final_g.reshape(1, D_MODEL)
    return _ffn(xm, p["g2"], p["wup"], p["cw_f"], p["cb_f"], p["wdn"], gf, final_g is not None)


def _trunk(x, layers, final_g):
    tabs = _dft_tables(x.shape[1])
    for l, p in enumerate(layers):
        x = _encoder_layer(x, p, tabs, final_g if l == len(layers) - 1 else None)
    return x


def kernel(x_prompt, x_sample, norm1_g, w_in, conv_w, conv_b, gate_b, filt_w1, filt_b1, filt_freq1, filt_w2,
           filt_b2, filt_freq2, filt_w3, hy_bias, ml_norm_g, w_out, norm2_g, w_up, ffn_conv_w, ffn_conv_b,
           w_down, final_g):
    weights = (norm1_g, w_in, conv_w, conv_b, gate_b, filt_w1, filt_b1, filt_freq1, filt_w2, filt_b2,
               filt_freq2, filt_w3, hy_bias, ml_norm_g, w_out, norm2_g, w_up, ffn_conv_w, ffn_conv_b, w_down)
    layers = [_layer_params(l, *weights) for l in range(norm1_g.shape[0])]
    return (_trunk(x_prompt, layers, final_g), _trunk(x_sample, layers, final_g))
```

```python
import functools
import math

import jax
import jax.numpy as jnp
import numpy as np
from jax import lax
from jax.experimental import pallas as pl
from jax.experimental.pallas import tpu as pltpu

F32 = jnp.float32
BF16 = jnp.bfloat16

LANES = 128
SUBLANES = 8

D_MODEL = 1024
HY_WIDTH = 512
ML_WIDTH = 512
ML_HEADS = 4
ML_HEAD_DIM = ML_WIDTH // ML_HEADS
HY_ORDER = 2
HY_EMB = 33
HY_BANDS = (HY_EMB - 1) // 2
HY_HIDDEN = 64
HY_DECAY_TARGET = 1e-2
HY_MAX_DECAY = math.log(HY_DECAY_TARGET) / 0.3
HY_MIN_DECAY = math.log(HY_DECAY_TARGET) / 1.5
D_FF = 2816
EPS = 1e-6

HY_COLS = 3 * HY_WIDTH
OFF_Q = HY_COLS
OFF_K = OFF_Q + ML_WIDTH
OFF_V = OFF_K + ML_WIDTH
OFF_O = OFF_V + ML_WIDTH
OFF_G = OFF_O + ML_WIDTH
N_GATE = 4 * ML_HEADS
CONV_CH = OFF_V

ML_CHUNK = 256
ML_OUT_CHUNKS = 8
ML_STATE_CHUNKS = 8
GATE_W = 4096
IN_TM = 1024
CONV_GW = 256
OUT_TM = 1024
FFN_TM = 1024
FFN_FT = 256
FILT_PC = 2048
FFT_ROWS = 2048
HY_PAIRS = 2


def _rms(xv, g):
    ms = jnp.mean(xv * xv, axis=-1, keepdims=True)
    return xv * lax.rsqrt(ms + EPS) * g


def _sigmoid(x):
    return 1.0 / (1.0 + jnp.exp(-x))


def _conv3_rows(p, w, b, tm):
    n = p.shape[0]
    dn = pltpu.roll(p, 1, 0)[SUBLANES:tm + SUBLANES]
    up = pltpu.roll(p, n - 1, 0)[SUBLANES:tm + SUBLANES]
    mid = p[SUBLANES:tm + SUBLANES]
    return dn * w[0:1] + mid * w[1:2] + up * w[2:3] + b


def _in_proj_body(xp_ref, x_ref, xn_ref, g1_ref, wc_ref, wvo_ref, cw_ref, cb_ref, gb_ref,
                  hyT_ref, gT_ref, q_ref, k_ref, v_ref, o_ref, *, tm, cbn):
    i = pl.program_id(1)
    nt = pl.num_programs(1)
    g1 = g1_ref[...]
    h = _rms(x_ref[0], g1)
    hp = jnp.where(i > 0, _rms(xp_ref[0], g1), 0.0)
    hn = jnp.where(i < nt - 1, _rms(xn_ref[0], g1), 0.0)
    hext = jnp.concatenate([hp, h, hn], axis=0).astype(BF16)
    for g in range(CONV_CH // CONV_GW):
        cs = slice(g * CONV_GW, (g + 1) * CONV_GW)
        pc = jnp.dot(hext, wc_ref[:, cs], preferred_element_type=F32)
        c = _conv3_rows(pc, cw_ref[:, cs], cb_ref[:, cs], tm)
        if g < HY_COLS // CONV_GW:
            cT = c.T
            for j in range(tm // LANES):
                for blk in range(CONV_GW // cbn):
                    hyT_ref[0, g * (CONV_GW // cbn) + blk, j * cbn:(j + 1) * cbn, :] = (
                        cT[blk * cbn:(blk + 1) * cbn, j * LANES:(j + 1) * LANES])
        elif g < OFF_K // CONV_GW:
            q_ref[0, g * CONV_GW - OFF_Q:(g + 1) * CONV_GW - OFF_Q, :] = (c * _sigmoid(c)).T.astype(BF16)
        else:
            k_ref[0, :, g * CONV_GW - OFF_K:(g + 1) * CONV_GW - OFF_K] = (
                c * _sigmoid(c) * (ML_HEAD_DIM ** -0.5)).astype(BF16)
    rT = lax.dot_general(wvo_ref[...], h.astype(BF16), (((1,), (1,)), ((), ())),
                         preferred_element_type=F32)
    v_ref[0] = rT[:ML_WIDTH].astype(BF16)
    o_ref[0] = _sigmoid(rT[ML_WIDTH:2 * ML_WIDTH])
    gT_ref[0] = rT[2 * ML_WIDTH:] + gb_ref[...]


def _hyena_channels(S):
    return min(FFT_ROWS // (2 * S // LANES), 32)


def _in_proj(x, g1, wc, wvo, cw, cb, gb):
    B, S, D = x.shape
    cbn = _hyena_channels(S)
    tm = min(IN_TM, S)
    nt = S // tm
    r8 = tm // SUBLANES
    full = lambda shape: pl.BlockSpec(shape, lambda b, i: (0,) * len(shape))
    return pl.pallas_call(
        functools.partial(_in_proj_body, tm=tm, cbn=cbn),
        grid=(B, nt),
        in_specs=[
            pl.BlockSpec((1, SUBLANES, D), lambda b, i: (b, jnp.maximum(i * r8 - 1, 0), 0)),
            pl.BlockSpec((1, tm, D), lambda b, i: (b, i, 0)),
            pl.BlockSpec((1, SUBLANES, D), lambda b, i: (b, jnp.minimum((i + 1) * r8, S // SUBLANES - 1), 0)),
            full((1, D)), full(wc.shape), full(wvo.shape), full(cw.shape), full(cb.shape), full(gb.shape),
        ],
        out_specs=[
            pl.BlockSpec((1, HY_COLS // cbn, (tm // LANES) * cbn, LANES), lambda b, i: (b, 0, i, 0)),
            pl.BlockSpec((1, N_GATE, tm), lambda b, i: (b, 0, i)),
            pl.BlockSpec((1, ML_WIDTH, tm), lambda b, i: (b, 0, i)),
            pl.BlockSpec((1, tm, ML_WIDTH), lambda b, i: (b, i, 0)),
            pl.BlockSpec((1, ML_WIDTH, tm), lambda b, i: (b, 0, i)),
            pl.BlockSpec((1, ML_WIDTH, tm), lambda b, i: (b, 0, i)),
        ],
        out_shape=[
            jax.ShapeDtypeStruct((B, HY_COLS // cbn, (S // LANES) * cbn, LANES), F32),
            jax.ShapeDtypeStruct((B, N_GATE, S), F32),
            jax.ShapeDtypeStruct((B, ML_WIDTH, S), BF16),
            jax.ShapeDtypeStruct((B, S, ML_WIDTH), BF16),
            jax.ShapeDtypeStruct((B, ML_WIDTH, S), BF16),
            jax.ShapeDtypeStruct((B, ML_WIDTH, S), F32),
        ],
        name="in_proj",
    )(x, x, x, g1, wc, wvo, cw, cb, gb)


def _dft_tables(L):
    N = 2 * L
    N1 = N // LANES
    H1 = N1 // 2

    def cs(num, den):
        ang = (2.0 * np.pi / den) * (num % den)
        return np.cos(ang), np.sin(ang)

    k1 = np.arange(N1, dtype=np.int64)[:, None]
    c, s = cs(k1 * np.arange(H1, dtype=np.int64)[None, :], N1)
    a1 = np.block([[c, s], [-s, c]])
    a1i = np.block([[c.T, -s.T], [s.T, c.T]]) * (1.0 / N)
    cf, sf = cs(k1 * np.arange(N1, dtype=np.int64)[None, :], N1)
    a1k = np.concatenate([cf, -sf], axis=0)
    ct, st = cs(k1 * np.arange(LANES, dtype=np.int64)[None, :], N)
    n2 = np.arange(LANES, dtype=np.int64)
    cg, sg = cs(n2[:, None] * n2[None, :], LANES)
    b2 = np.block([[cg, -sg], [sg, cg]])
    b2c = np.block([[cg, sg], [-sg, cg]])
    tabs = dict(a1=a1, a1i=a1i, a1k=a1k, twr=ct, twi=-st, b2=b2, b2c=b2c)
    tabs = {k: jnp.asarray(v.astype(np.float32)) for k, v in tabs.items()}
    return dict(N1=N1, H1=H1, **tabs)


def _filter_features(L):
    t = np.linspace(0.0, 1.0, L)[:, None]
    freqs = np.linspace(1e-4, HY_BANDS - 1, HY_BANDS)
    ang = (2.0 * np.pi / L) * np.arange(L, dtype=np.float64)[:, None] * freqs[None, :]
    z = np.concatenate([t, np.cos(ang), -np.sin(ang)], axis=-1)
    z_all = np.concatenate([z, z[0:1], z[L - 1:0:-1]], axis=0)
    t_all = np.concatenate([t, t[0:1], t[L - 1:0:-1]], axis=0)
    zT = np.pad(z_all, ((0, 0), (0, HY_HIDDEN - HY_EMB))).T
    return jnp.asarray(zT.astype(np.float32)), jnp.asarray(t_all.T.astype(np.float32))


def _split_bf16(x):
    hi = x.astype(BF16)
    return hi, (x - hi.astype(F32)).astype(BF16)


def _dot_split(w3, x):
    hi, lo = _split_bf16(x)
    return jnp.dot(w3, jnp.concatenate([hi, lo, hi], axis=0), preferred_element_type=F32)


def _split_cols(w):
    hi, lo = _split_bf16(w)
    return jnp.concatenate([hi, hi, lo], axis=-1)


def _filt_mlp_body(zT_ref, w1T_ref, b1_ref, f1_ref, w2T_ref, b2_ref, f2_ref, h2_ref):
    h1 = jnp.sin(f1_ref[...] * (_dot_split(w1T_ref[...], zT_ref[...]) + b1_ref[...]))
    h2_ref[...] = jnp.sin(f2_ref[...] * (_dot_split(w2T_ref[...], h1) + b2_ref[...]))


def _filt_body(h2_ref, t_ref, w3T_ref, dl_ref, kT_ref, sc_ref, *, L, pc):
    p = pl.program_id(1)
    last = pl.num_programs(1) - 1
    w3 = w3T_ref[0].reshape(HY_ORDER * LANES, 3 * HY_HIDDEN)
    h3 = _dot_split(w3, h2_ref[...])
    win = jnp.exp(-t_ref[...] * dl_ref[...])
    n_idx = p * pc + lax.broadcasted_iota(jnp.int32, (1, pc), 1)
    win = jnp.where(n_idx == L, 0.0, win)
    kern = h3 * jnp.concatenate([win] * HY_ORDER, axis=0)
    for o in range(HY_ORDER):
        for j in range(pc // LANES):
            kT_ref[o, j] = kern[o * LANES:(o + 1) * LANES, j * LANES:(j + 1) * LANES]
    ss = jnp.sum(kern * kern, axis=1, keepdims=True)

    @pl.when(p == 0)
    def _():
        sc_ref[0] = ss

    @pl.when(jnp.logical_and(p > 0, p < last))
    def _():
        sc_ref[0] = sc_ref[0] + ss

    @pl.when(p == last)
    def _():
        sc_ref[0] = lax.rsqrt(sc_ref[0] + ss)


def _filt(L, zT, t_all, w1T, b1, f1, w2T, b2, f2, w3T, dl):
    pc = min(FILT_PC, L)
    npc = 2 * L // pc
    ncb = HY_WIDTH // LANES
    full1 = lambda a: pl.BlockSpec(a.shape, lambda p: (0,) * a.ndim)
    h2T = pl.pallas_call(
        _filt_mlp_body,
        grid=(npc,),
        in_specs=[pl.BlockSpec((HY_HIDDEN, pc), lambda p: (0, p)),
                  full1(w1T), full1(b1), full1(f1), full1(w2T), full1(b2), full1(f2)],
        out_specs=pl.BlockSpec((HY_HIDDEN, pc), lambda p: (0, p)),
        out_shape=jax.ShapeDtypeStruct((HY_HIDDEN, 2 * L), F32),
        name="hyena_filter_mlp",
    )(zT, w1T, b1, f1, w2T, b2, f2)
    return pl.pallas_call(
        functools.partial(_filt_body, L=L, pc=pc),
        grid=(ncb, npc),
        in_specs=[
            pl.BlockSpec((HY_HIDDEN, pc), lambda c, p: (0, p)),
            pl.BlockSpec((1, pc), lambda c, p: (0, p)),
            pl.BlockSpec((1, HY_ORDER, LANES, 3 * HY_HIDDEN), lambda c, p: ((p * pc) // L, 0, c, 0)),
            pl.BlockSpec((LANES, 1), lambda c, p: (c, 0)),
        ],
        out_specs=[
            pl.BlockSpec((HY_ORDER, pc // LANES, LANES, LANES), lambda c, p: (0, p, c, 0)),
            pl.BlockSpec((1, HY_ORDER * LANES, 1), lambda c, p: (c, 0, 0)),
        ],
        out_shape=[
            jax.ShapeDtypeStruct((HY_ORDER, 2 * L // LANES, HY_WIDTH, LANES), F32),
            jax.ShapeDtypeStruct((ncb, HY_ORDER * LANES, 1), F32),
        ],
        name="hyena_filter",
    )(h2T, t_all, w3T, dl)


def _kfft_body(sc_ref, kT_ref, a1k_ref, twr_ref, twi_ref, b2_ref, kr_ref, ki_ref, x_s, l2_s, *, N1, cbn):
    o = pl.program_id(0)
    cb = pl.program_id(1)
    ks = jnp.swapaxes(kT_ref[...], 0, 1)
    for c in range(cbn):
        x_s[:, c * LANES:(c + 1) * LANES] = (ks[c] * sc_ref[o * HY_WIDTH + cb * cbn + c]).astype(BF16)
    y = jnp.dot(a1k_ref[...], x_s[...], preferred_element_type=F32)
    twr = twr_ref[...]
    twi = twi_ref[...]
    for c in range(cbn):
        yr = y[0:N1, c * LANES:(c + 1) * LANES]
        yi = y[N1:2 * N1, c * LANES:(c + 1) * LANES]
        rs = slice(c * N1, (c + 1) * N1)
        l2_s[rs, 0:LANES] = (yr * twr - yi * twi).astype(BF16)
        l2_s[rs, LANES:2 * LANES] = (yr * twi + yi * twr).astype(BF16)
    z = jnp.dot(l2_s[...], b2_ref[...], preferred_element_type=F32)
    kr_ref[...] = z[:, :LANES].astype(BF16)
    ki_ref[...] = z[:, LANES:].astype(BF16)


def _kfft(scale, kT, tabs):
    N1 = tabs["N1"]
    cbn = _hyena_channels(N1 * LANES // 2)
    ncb = HY_WIDTH // cbn
    full = lambda a: pl.BlockSpec(a.shape, lambda o, c: (0,) * a.ndim)
    twr, twi = tabs["twr"], tabs["twi"]
    a1k = tabs["a1k"].astype(BF16)
    b2 = tabs["b2"].astype(BF16)
    return pl.pallas_call(
        functools.partial(_kfft_body, N1=N1, cbn=cbn),
        grid=(HY_ORDER, ncb),
        in_specs=[
            pl.BlockSpec(memory_space=pltpu.SMEM),
            pl.BlockSpec((None, N1, cbn, LANES), lambda o, c: (o, 0, c, 0)),
            full(a1k), full(twr), full(twi), full(b2),
        ],
        out_specs=[
            pl.BlockSpec((None, cbn * N1, LANES), lambda o, c: (o, c, 0)),
            pl.BlockSpec((None, cbn * N1, LANES), lambda o, c: (o, c, 0)),
        ],
        out_shape=[
            jax.ShapeDtypeStruct((HY_ORDER, HY_WIDTH * N1, LANES), BF16),
            jax.ShapeDtypeStruct((HY_ORDER, HY_WIDTH * N1, LANES), BF16),
        ],
        scratch_shapes=[pltpu.VMEM((N1, cbn * LANES), BF16), pltpu.VMEM((cbn * N1, 2 * LANES), BF16)],
        name="hyena_kfft",
    )(scale, kT, a1k, twr, twi, b2)


def _hyena_body(hyb_ref, x1_ref, x2_ref, v_ref, kr_ref, ki_ref, a1_ref, twr_ref, twi_ref,
                b2_ref, b2c_ref, a1i_ref, o_ref, x_s, l2_s, r1_s, z_s, *, N1, H1, cbn):
    cb = pl.program_id(0)

    def slabs(ref, pi, e):
        return jnp.swapaxes(ref[pi, e].reshape(H1, cbn, LANES), 0, 1)

    def long_conv(order):
        twr = twr_ref[...]
        twi = twi_ref[...]
        y = jnp.dot(a1_ref[...], x_s[...], preferred_element_type=F32).astype(BF16)
        for c in range(cbn):
            yr = y[0:N1, c * LANES:(c + 1) * LANES]
            yi = y[N1:2 * N1, c * LANES:(c + 1) * LANES]
            l2_s[c * N1:(c + 1) * N1, 0:LANES] = yr * twr - yi * twi
            l2_s[c * N1:(c + 1) * N1, LANES:2 * LANES] = yr * twi + yi * twr
        z = jnp.dot(l2_s[...], b2_ref[...], preferred_element_type=F32).astype(BF16)
        zr = z[:, :LANES]
        zi = z[:, LANES:]
        kr = kr_ref[order]
        ki = ki_ref[order]
        l2_s[:, 0:LANES] = zr * kr - zi * ki
        l2_s[:, LANES:2 * LANES] = zr * ki + zi * kr
        v = jnp.dot(l2_s[...], b2c_ref[...], preferred_element_type=F32).astype(BF16)
        for c in range(cbn):
            vr = v[c * N1:(c + 1) * N1, :LANES]
            vi = v[c * N1:(c + 1) * N1, LANES:]
            r1_s[0:N1, c * LANES:(c + 1) * LANES] = vr * twr + vi * twi
            r1_s[N1:2 * N1, c * LANES:(c + 1) * LANES] = vi * twr - vr * twi
        return jnp.dot(a1i_ref[...], r1_s[...], preferred_element_type=F32)

    for pi in range(o_ref.shape[0]):
        for e in range(2):
            vs = slabs(v_ref, pi, e)
            z_s[e] = vs
            for c in range(cbn):
                x_s[e * H1:(e + 1) * H1, c * LANES:(c + 1) * LANES] = vs[c].astype(BF16)
        y1 = long_conv(0)
        for e in range(2):
            x1s = slabs(x1_ref, pi, e)
            for c in range(cbn):
                c1 = y1[e * H1:(e + 1) * H1, c * LANES:(c + 1) * LANES] + hyb_ref[cb * cbn + c] * z_s[e, c]
                z2 = x1s[c] * c1
                z_s[e, c] = z2
                x_s[e * H1:(e + 1) * H1, c * LANES:(c + 1) * LANES] = z2.astype(BF16)
        y2 = long_conv(1)
        for e in range(2):
            x2s = slabs(x2_ref, pi, e)
            for c in range(cbn):
                c2 = (y2[e * H1:(e + 1) * H1, c * LANES:(c + 1) * LANES]
                      + hyb_ref[HY_WIDTH + cb * cbn + c] * z_s[e, c])
                o_ref[pi, e, c] = x2s[c] * c2


def _hyena(hyT, hyb, kr, ki, tabs):
    B = hyT.shape[0]
    N1 = tabs["N1"]
    H1 = N1 // 2
    cbn = _hyena_channels(H1 * LANES)
    assert hyT.shape == (B, HY_COLS // cbn, H1 * cbn, LANES) and B % 2 == 0
    ncb = HY_WIDTH // cbn
    hy5 = hyT.reshape(B // 2, 2, HY_COLS // cbn, H1 * cbn, LANES)
    a1 = tabs["a1"].astype(BF16)
    a1i = tabs["a1i"].astype(BF16)
    b2 = tabs["b2"].astype(BF16)
    b2c = tabs["b2c"].astype(BF16)
    twr, twi = tabs["twr"].astype(BF16), tabs["twi"].astype(BF16)
    full = lambda a: pl.BlockSpec(a.shape, lambda c, p: (0,) * a.ndim)
    smem = pl.BlockSpec(memory_space=pltpu.SMEM)
    npp = HY_PAIRS if (B // 2) % HY_PAIRS == 0 else 1
    xspec = lambda k: pl.BlockSpec((npp, 2, None, H1 * cbn, LANES), lambda c, p: (p, 0, k * ncb + c, 0, 0))
    kspec = pl.BlockSpec((HY_ORDER, cbn * N1, LANES), lambda c, p: (0, c, 0))
    out = pl.pallas_call(
        functools.partial(_hyena_body, N1=N1, H1=H1, cbn=cbn),
        grid=(ncb, B // 2 // npp),
        in_specs=[smem, xspec(0), xspec(1), xspec(2), kspec, kspec,
                  full(a1), full(twr), full(twi), full(b2), full(b2c), full(a1i)],
        out_specs=pl.BlockSpec((npp, 2, cbn, H1, LANES), lambda c, p: (p, 0, c, 0, 0)),
        out_shape=jax.ShapeDtypeStruct((B // 2, 2, HY_WIDTH, H1, LANES), F32),
        scratch_shapes=[
            pltpu.VMEM((2 * H1, cbn * LANES), BF16),
            pltpu.VMEM((cbn * N1, 2 * LANES), BF16),
            pltpu.VMEM((2 * N1, cbn * LANES), BF16),
            pltpu.VMEM((2, cbn, H1, LANES), F32),
        ],
        name="hyena_mix",
    )(hyb, hy5, hy5, hy5, kr, ki, a1, twr, twi, b2, b2c, a1i)
    return out.reshape(B, HY_WIDTH, H1, LANES)


GP_ROWS = 6 * ML_HEADS


def _seg_scan(x, op, reverse, fill, T):
    W = x.shape[1]
    pos = lax.broadcasted_iota(jnp.int32, x.shape, 1) % T
    k = 1
    while k < T:
        if reverse:
            x = op(x, jnp.where(pos < T - k, pltpu.roll(x, W - k, 1), fill))
        else:
            x = op(x, jnp.where(pos >= k, pltpu.roll(x, k, 1), fill))
        k *= 2
    return x


def _log_sigmoid(x):
    return jnp.minimum(x, 0.0) - jnp.log(1.0 + jnp.exp(-jnp.abs(x)))


def _gate_prep_body(g_ref, o_ref, *, T):
    g = g_ref[0]
    for d in range(2):
        ig = g[8 * d:8 * d + ML_HEADS]
        fg = g[8 * d + ML_HEADS:8 * d + 2 * ML_HEADS]
        b = _seg_scan(_log_sigmoid(fg), jnp.add, d == 1, 0.0, T)
        a = ig - b
        o_ref[0, 4 * d:4 * d + 4, :] = a
        o_ref[0, 8 + 4 * d:12 + 4 * d, :] = b
        o_ref[0, 16 + 4 * d:20 + 4 * d, :] = _seg_scan(a, jnp.maximum, d == 1, -jnp.inf, T)


def _gate_prep(gT, T):
    B, _, S = gT.shape
    w = min(GATE_W, S)
    return pl.pallas_call(
        functools.partial(_gate_prep_body, T=T),
        grid=(B, S // w),
        in_specs=[pl.BlockSpec((1, N_GATE, w), lambda b, i: (b, 0, i))],
        out_specs=pl.BlockSpec((1, GP_ROWS, w), lambda b, i: (b, 0, i)),
        out_shape=jax.ShapeDtypeStruct((B, GP_ROWS, S), F32),
        name="mlstm_gates",
    )(gT)


def _gp_dir(gp, d, T):
    a = gp[4 * d:4 * d + 4]
    b = gp[8 + 4 * d:12 + 4 * d]
    cm = gp[16 + 4 * d:20 + 4 * d]
    e = 0 if d == 1 else T - 1
    return a, b, cm, b[:, e:e + 1], cm[:, e:e + 1]


def _mlstm_state_body(gf_ref, gb_ref, kf_ref, vf_ref, kb_ref, vb_ref, sf_ref, mf_ref, sb_ref, mb_ref,
                      s_s, m_s, *, T):
    i = pl.program_id(1)

    @pl.when(i == 0)
    def _():
        s_s[...] = jnp.zeros(s_s.shape, F32)
        m_s[...] = jnp.zeros(m_s.shape, F32)

    cps = gf_ref.shape[2] // T
    dirs = ((gf_ref, kf_ref, vf_ref, sf_ref, mf_ref), (gb_ref, kb_ref, vb_ref, sb_ref, mb_ref))
    loc = {}
    for d, (g_ref, k_ref, v_ref, _, _) in enumerate(dirs):
        for u in range(cps):
            ts = slice(u * T, (u + 1) * T)
            a, _, _, btot, amax = _gp_dir(g_ref[0, :, ts], d, T)
            w = jnp.exp(a - amax)
            cs = []
            for h in range(ML_HEADS):
                hs = slice(h * ML_HEAD_DIM, (h + 1) * ML_HEAD_DIM)
                wr = w[h:h + 1, :]
                vw = jnp.concatenate([(v_ref[0, hs, ts].astype(F32) * wr).astype(BF16),
                                      jnp.broadcast_to(wr, (ML_HEAD_DIM, T)).astype(BF16)], axis=0)
                cs.append(jnp.dot(vw, k_ref[0, ts, hs], preferred_element_type=F32))
            loc[d, u] = (btot, amax, cs)
    for d, (_, _, _, s_out, m_out) in enumerate(dirs):
        m_old = m_s[d, 0:ML_HEADS, 0:1]
        sts = [s_s[d, h] for h in range(ML_HEADS)]
        for u in (range(cps) if d == 0 else reversed(range(cps))):
            btot, amax, cs = loc[d, u]
            m_out[0, u] = jnp.broadcast_to(jnp.concatenate([m_old, m_old], axis=0), (SUBLANES, LANES))
            m_new = jnp.maximum(btot + m_old, btot + amax)
            s_old = jnp.exp(btot + m_old - m_new)
            s_loc = jnp.exp(btot + amax - m_new)
            for h in range(ML_HEADS):
                s_out[0, u, h] = sts[h].astype(BF16)
                sts[h] = s_old[h:h + 1, :] * sts[h] + s_loc[h:h + 1, :] * cs[h]
            m_old = m_new
        m_s[d, 0:ML_HEADS, :] = jnp.broadcast_to(m_old, (ML_HEADS, LANES))
        for h in range(ML_HEADS):
            s_s[d, h] = sts[h]


def _mlstm_state(gp, k, vT, T):
    B, S, _ = k.shape
    nc = S // T
    sd = 2 * ML_HEAD_DIM
    cps = ML_STATE_CHUNKS if nc % ML_STATE_CHUNKS == 0 else 1
    ns = nc // cps
    tw = cps * T
    pick = lambda rev, f: (lambda b, i: f(b, ns - 1 - i)) if rev else (lambda b, i: f(b, i))
    gspec = lambda rev: pl.BlockSpec((1, GP_ROWS, tw), pick(rev, lambda b, c: (b, 0, c)))
    kspec = lambda rev: pl.BlockSpec((1, tw, ML_WIDTH), pick(rev, lambda b, c: (b, c, 0)))
    vspec = lambda rev: pl.BlockSpec((1, ML_WIDTH, tw), pick(rev, lambda b, c: (b, 0, c)))
    sspec = lambda rev: pl.BlockSpec((1, cps, ML_HEADS, sd, ML_HEAD_DIM), pick(rev, lambda b, c: (b, c, 0, 0, 0)))
    mspec = lambda rev: pl.BlockSpec((1, cps, SUBLANES, LANES), pick(rev, lambda b, c: (b, c, 0, 0)))
    return pl.pallas_call(
        functools.partial(_mlstm_state_body, T=T),
        grid=(B, ns),
        in_specs=[gspec(False), gspec(True), kspec(False), vspec(False), kspec(True), vspec(True)],
        out_specs=[sspec(False), mspec(False), sspec(True), mspec(True)],
        out_shape=[
            jax.ShapeDtypeStruct((B, nc, ML_HEADS, sd, ML_HEAD_DIM), BF16),
            jax.ShapeDtypeStruct((B, nc, SUBLANES, LANES), F32),
            jax.ShapeDtypeStruct((B, nc, ML_HEADS, sd, ML_HEAD_DIM), BF16),
            jax.ShapeDtypeStruct((B, nc, SUBLANES, LANES), F32),
        ],
        scratch_shapes=[pltpu.VMEM((2, ML_HEADS, sd, ML_HEAD_DIM), F32), pltpu.VMEM((2, SUBLANES, LANES), F32)],
        name="mlstm_state",
    )(gp, gp, k, vT, k, vT)


def _mlstm_out_body(g_ref, q_ref, k_ref, v_ref, o_ref, sf_ref, mf_ref, sb_ref, mb_ref, ng_ref, y_ref, *, T):
    for u in range(g_ref.shape[2] // T):
        _mlstm_out_chunk(u, g_ref, q_ref, k_ref, v_ref, o_ref, sf_ref, mf_ref, sb_ref, mb_ref, ng_ref, y_ref, T)


def _mlstm_out_chunk(u, g_ref, q_ref, k_ref, v_ref, o_ref, sf_ref, mf_ref, sb_ref, mb_ref, ng_ref, y_ref, T):
    ts = slice(u * T, (u + 1) * T)
    gp = g_ref[0, :, ts]
    rows = []
    for d, m_ref in enumerate((mf_ref, mb_ref)):
        a, b, cm, _, _ = _gp_dir(gp, d, T)
        m = m_ref[0, u][0:ML_HEADS, 0:1]
        mt = jnp.maximum(cm, m)
        rows.append((a, mt, jnp.exp(-(b + mt)), jnp.exp(m - mt)))
    a_cols = jnp.concatenate([rows[0][0], rows[1][0],
                              jnp.zeros((LANES - 2 * ML_HEADS, T), F32)], axis=0).T
    ss = lax.broadcasted_iota(jnp.int32, (T, T), 0)
    tt = lax.broadcasted_iota(jnp.int32, (T, T), 1)
    masks = (ss <= tt, ss >= tt)
    ones = jnp.ones((ML_HEAD_DIM, T), BF16)
    hsl = [slice(h * ML_HEAD_DIM, (h + 1) * ML_HEAD_DIM) for h in range(ML_HEADS)]
    ps, qss, v1s = [], [], []
    for h in range(ML_HEADS):
        qT = q_ref[0, hsl[h], ts]
        ps.append(jnp.dot(k_ref[0, ts, hsl[h]], qT, preferred_element_type=F32))
        s_fb = jnp.concatenate([sf_ref[0, u, h], sb_ref[0, u, h]], axis=0)
        qss.append(jnp.dot(s_fb, qT, preferred_element_type=F32))
        v1s.append(jnp.concatenate([v_ref[0, hsl[h], ts], ones], axis=0))
    sms = []
    for h in range(ML_HEADS):
        for d in range(2):
            a_c = a_cols[:, ML_HEADS * d + h:ML_HEADS * d + h + 1]
            e = jnp.exp(jnp.where(masks[d], a_c - rows[d][1][h:h + 1, :], -jnp.inf))
            sms.append((ps[h] * e).astype(BF16))
    accs = [jnp.dot(v1s[h], sms[2 * h + d], preferred_element_type=F32)
            for h in range(ML_HEADS) for d in range(2)]
    sd = 2 * ML_HEAD_DIM
    for h in range(ML_HEADS):
        hm = None
        for d in range(2):
            emt_r = rows[d][2][h:h + 1, :]
            int_r = rows[d][3][h:h + 1, :]
            acc = accs[2 * h + d] + int_r * qss[h][sd * d:sd * (d + 1)]
            den = acc[ML_HEAD_DIM:ML_HEAD_DIM + SUBLANES]
            rden = 1.0 / jnp.maximum(jnp.abs(den), emt_r)
            hd = acc[:ML_HEAD_DIM] * jnp.concatenate([rden] * (ML_HEAD_DIM // SUBLANES), axis=0)
            hm = hd if hm is None else hm + hd
        hn = hm * lax.rsqrt(jnp.mean(hm * hm, axis=0, keepdims=True) + EPS)
        y_ref[0, hsl[h], ts] = (hn * ng_ref[hsl[h], :] * o_ref[0, hsl[h], ts]).astype(BF16)


def _mlstm_out(gp, qT, k, vT, oT, sf, mf, sb, mb, ng, T):
    B, S, _ = k.shape
    nc = S // T
    sd = 2 * ML_HEAD_DIM
    cps = ML_OUT_CHUNKS if nc % ML_OUT_CHUNKS == 0 else 1
    tw = cps * T
    tspec = pl.BlockSpec((1, ML_WIDTH, tw), lambda b, i: (b, 0, i))
    sspec = pl.BlockSpec((1, cps, ML_HEADS, sd, ML_HEAD_DIM), lambda b, i: (b, i, 0, 0, 0))
    mspec = pl.BlockSpec((1, cps, SUBLANES, LANES), lambda b, i: (b, i, 0, 0))
    return pl.pallas_call(
        functools.partial(_mlstm_out_body, T=T),
        grid=(B, nc // cps),
        in_specs=[pl.BlockSpec((1, GP_ROWS, tw), lambda b, i: (b, 0, i)), tspec,
                  pl.BlockSpec((1, tw, ML_WIDTH), lambda b, i: (b, i, 0)), tspec, tspec,
                  sspec, mspec, sspec, mspec, pl.BlockSpec((ML_WIDTH, 1), lambda b, i: (0, 0))],
        out_specs=tspec,
        out_shape=jax.ShapeDtypeStruct((B, ML_WIDTH, S), BF16),
        name="mlstm_out",
    )(gp, qT, k, vT, oT, sf, mf, sb, mb, ng)


def _out_proj_body(x_ref, yhy_ref, yml_ref, why_ref, wml_ref, o_ref, *, tm):
    yh = jnp.swapaxes(yhy_ref[0], 0, 1)
    yt = jnp.concatenate([yh[j].T for j in range(tm // LANES)], axis=0)
    acc = jnp.dot(yt.astype(BF16), why_ref[...], preferred_element_type=F32)
    acc = acc + jnp.dot(yml_ref[0].astype(F32).T.astype(BF16), wml_ref[...], preferred_element_type=F32)
    o_ref[0] = x_ref[0] + acc


def _out_proj(x, yhy, yml, why, wml):
    B, S, D = x.shape
    tm = min(OUT_TM, S)
    full = lambda a: pl.BlockSpec(a.shape, lambda b, i: (0,) * a.ndim)
    return pl.pallas_call(
        functools.partial(_out_proj_body, tm=tm),
        grid=(B, S // tm),
        in_specs=[
            pl.BlockSpec((1, tm, D), lambda b, i: (b, i, 0)),
            pl.BlockSpec((1, HY_WIDTH, tm // LANES, LANES), lambda b, i: (b, 0, i, 0)),
            pl.BlockSpec((1, ML_WIDTH, tm), lambda b, i: (b, 0, i)),
            full(why), full(wml),
        ],
        out_specs=pl.BlockSpec((1, tm, D), lambda b, i: (b, i, 0)),
        out_shape=jax.ShapeDtypeStruct((B, S, D), F32),
        name="out_proj",
    )(x, yhy, yml, why, wml)


def _ffn_body(xp_ref, x_ref, xn_ref, g2_ref, wup_ref, cw_ref, cb_ref, wdn_ref, gf_ref, o_ref, act_s,
              *, tm, ft, final_norm):
    i = pl.program_id(1)
    nt = pl.num_programs(1)
    g2 = g2_ref[...]
    xm = x_ref[0]
    h = _rms(xm, g2)
    hp = jnp.where(i > 0, _rms(xp_ref[0], g2), 0.0)
    hn = jnp.where(i < nt - 1, _rms(xn_ref[0], g2), 0.0)
    hext = jnp.concatenate([hp, h, hn], axis=0).astype(BF16)
    for j in range(D_FF // ft):
        cv = slice(j * ft, (j + 1) * ft)
        cg = slice(D_FF + j * ft, D_FF + (j + 1) * ft)
        uv = jnp.dot(hext, wup_ref[:, cv], preferred_element_type=F32)
        ug = jnp.dot(hext, wup_ref[:, cg], preferred_element_type=F32)
        val = _conv3_rows(uv, cw_ref[:, cv], cb_ref[:, cv], tm)
        gate = _conv3_rows(ug, cw_ref[:, cg], cb_ref[:, cg], tm)
        act_s[:, cv] = (gate * _sigmoid(gate) * val).astype(BF16)
    hh = tm // 2
    for r in range(2):
        rs = slice(r * hh, (r + 1) * hh)
        xo = xm[rs] + jnp.dot(act_s[rs, :], wdn_ref[...], preferred_element_type=F32)
        o_ref[0, rs, :] = _rms(xo, gf_ref[...]) if final_norm else xo


def _ffn(x, g2, wup, cw, cb, wdn, gf, final_norm):
    B, S, D = x.shape
    tm = min(FFN_TM, S)
    nt = S // tm
    r8 = tm // SUBLANES
    full = lambda a: pl.BlockSpec(a.shape, lambda b, i: (0,) * a.ndim)
    once = lambda a: pl.BlockSpec(a.shape, lambda b, i: (0,) * a.ndim, pipeline_mode=pl.Buffered(1))
    return pl.pallas_call(
        functools.partial(_ffn_body, tm=tm, ft=FFN_FT, final_norm=final_norm),
        grid=(B, nt),
        in_specs=[
            pl.BlockSpec((1, SUBLANES, D), lambda b, i: (b, jnp.maximum(i * r8 - 1, 0), 0)),
            pl.BlockSpec((1, tm, D), lambda b, i: (b, i, 0)),
            pl.BlockSpec((1, SUBLANES, D), lambda b, i: (b, jnp.minimum((i + 1) * r8, S // SUBLANES - 1), 0)),
            full(g2), once(wup), full(cw), full(cb), once(wdn), full(gf),
        ],
        out_specs=pl.BlockSpec((1, tm, D), lambda b, i: (b, i, 0)),
        out_shape=jax.ShapeDtypeStruct((B, S, D), F32),
        scratch_shapes=[pltpu.VMEM((tm, D_FF), BF16)],
        name="ffn",
    )(x, x, x, g2, wup, cw, cb, wdn, gf)


def _layer_params(l, norm1_g, w_in, conv_w, conv_b, gate_b, filt_w1, filt_b1, filt_freq1, filt_w2, filt_b2,
                  filt_freq2, filt_w3, hy_bias, ml_norm_g, w_out, norm2_g, w_up, ffn_conv_w, ffn_conv_b, w_down):
    w = w_in[l]
    col = lambda a: a.astype(F32).reshape(-1, 1)
    return dict(
        g1=norm1_g[l].reshape(1, D_MODEL),
        wc=w[:, :CONV_CH].astype(BF16),
        wvo=w[:, OFF_V:].T.astype(BF16),
        cw=conv_w[l], cb=conv_b[l].reshape(1, -1),
        gb=col(gate_b[l]),
        hyb=hy_bias[l].reshape(-1),
        w1T=_split_cols(jnp.pad(filt_w1[l], ((0, HY_HIDDEN - HY_EMB), (0, 0))).T),
        b1=col(filt_b1[l]), f1=col(filt_freq1[l]),
        w2T=_split_cols(filt_w2[l].T), b2=col(filt_b2[l]), f2=col(filt_freq2[l]),
        w3T=_split_cols(filt_w3[l].T).reshape(2, HY_ORDER, HY_WIDTH, 3 * HY_HIDDEN),
        ng=col(ml_norm_g[l]),
        why_o=w_out[l][:HY_WIDTH].astype(BF16), wml_o=w_out[l][HY_WIDTH:].astype(BF16),
        g2=norm2_g[l].reshape(1, D_MODEL),
        wup=w_up[l].astype(BF16), cw_f=ffn_conv_w[l], cb_f=ffn_conv_b[l].reshape(1, -1),
        wdn=w_down[l].astype(BF16),
    )


def _hyena_spectrum(L, p, tabs):
    zT, t_all = _filter_features(L)
    deltas = jnp.asarray(np.abs(np.linspace(HY_MIN_DECAY, HY_MAX_DECAY, HY_WIDTH)).astype(np.float32).reshape(-1, 1))
    kT, sc = _filt(L, zT, t_all, p["w1T"], p["b1"], p["f1"], p["w2T"], p["b2"], p["f2"], p["w3T"], deltas)
    ncb = HY_WIDTH // LANES
    scale = sc.reshape(ncb, HY_ORDER, LANES).transpose(1, 0, 2).reshape(-1)
    return _kfft(scale, kT, tabs)


def _encoder_layer(x, p, tabs, final_g):
    B, S, _ = x.shape
    hyT, gT, q, k, v, o = _in_proj(x, p["g1"], p["wc"], p["wvo"], p["cw"], p["cb"], p["gb"])
    kr, ki = _hyena_spectrum(S, p, tabs)
    y_hy = _hyena(hyT, p["hyb"], kr, ki, tabs)
    T = min(ML_CHUNK, S)
    gp = _gate_prep(gT, T)
    sf, mf, sb, mb = _mlstm_state(gp, k, v, T)
    y_ml = _mlstm_out(gp, q, k, v, o, sf, mf, sb, mb, p["ng"], T)
    xm = _out_proj(x, y_hy, y_ml, p["why_o"], p["wml_o"])
    gf = p["g2"] if final_g is None else final_g.reshape(1, D_MODEL)
    return _ffn(xm, p["g2"], p["wup"], p["cw_f"], p["cb_f"], p["wdn"], gf, final_g is not None)


def _trunk(x, layers, final_g):
    tabs = _dft_tables(x.shape[1])
    for l, p in enumerate(layers):
        x = _encoder_layer(x, p, tabs, final_g if l == len(layers) - 1 else None)
    return x


def kernel(x_prompt, x_sample, norm1_g, w_in, conv_w, conv_b, gate_b, filt_w1, filt_b1, filt_freq1, filt_w2,
           filt_b2, filt_freq2, filt_w3, hy_bias, ml_norm_g, w_out, norm2_g, w_up, ffn_conv_w, ffn_conv_b,
           w_down, final_g):
    weights = (norm1_g, w_in, conv_w, conv_b, gate_b, filt_w1, filt_b1, filt_freq1, filt_w2, filt_b2,
               filt_freq2, filt_w3, hy_bias, ml_norm_g, w_out, norm2_g, w_up, ffn_conv_w, ffn_conv_b, w_down)
    layers = [_layer_params(l, *weights) for l in range(norm1_g.shape[0])]
    return (_trunk(x_prompt, layers, final_g), _trunk(x_sample, layers, final_g))
```
